```python
import math
import jax, jax.numpy as jnp
from jax import lax
import numpy as np

D_MODEL = 2048
BATCH = 1
SEQ = 8192
DEPTH = 2
DEC_BATCH = 128
DEC_SEQ = 8
PAST_LEN = 2048
PAGE_SIZE = 128

N_EVEN = (DEPTH + 1) // 2
N_ODD = DEPTH // 2
CONV_W = 3
NORM_EPS = 1e-6
D_A = D_MODEL // 2
P_A = 3 * D_A
D_B = D_MODEL // 2
HD_B = 64
H_B = D_B // HD_B
R_W = 64
R_A = 64
R_G = 160
P_B = 3 * D_B + R_W + R_A + R_G
P_AB = P_A + P_B
GN_EPS_B = 64e-5
HD_C = 128
H_C = D_MODEL // (2 * HD_C)
DV_C = 2 * HD_C
D_QK_C = 2 * H_C * HD_C
D_V_C = H_C * DV_C
ATTN_SCALE = HD_C ** -0.5
SUBLN_EPS = 1e-5
Q_BLOCK = 128
D_FF = 5632

kernel_name = 'hybrid_shortconv_rwkv7_diffattn_convffn_step'


def rms_norm(x, g, eps=NORM_EPS):
    xf = x.astype(jnp.float32)
    y = xf * lax.rsqrt(jnp.mean(xf * xf, axis=-1, keepdims=True) + eps)
    return (y * g.astype(jnp.float32)).astype(x.dtype)


def causal_dwconv(u, buf, w):
    T = u.shape[1]
    full = jnp.concatenate([buf.astype(u.dtype), u], axis=1)
    y = w[0] * full[:, 0:T]
    for i in range(1, CONV_W):
        y = y + w[i] * full[:, i:i + T]
    return y, full[:, T:]


def token_shift(p, prev):
    full = jnp.concatenate([prev[:, None].astype(p.dtype), p], axis=1)
    return full[:, :-1], full[:, -1]


def wkv7_recurrence(r, decay, k, v, kk, b, s0):
    def step(s, inp):
        r_t, w_t, k_t, v_t, kk_t, b_t = inp
        sa = jnp.einsum('bhvk,bhk->bhv', s, -kk_t)
        s = s * w_t[:, :, None, :] + sa[..., None] * b_t[:, :, None, :] + v_t[..., None] * k_t[:, :, None, :]
        return s, jnp.einsum('bhvk,bhk->bhv', s, r_t)
    xs = tuple(jnp.moveaxis(t.astype(jnp.float32), 1, 0) for t in (r, decay, k, v, kk, b))
    s_final, ys = lax.scan(step, s0.astype(jnp.float32), xs)
    return jnp.moveaxis(ys, 0, 1), s_final


def mixer_ab(h, conv_buf, shift_prev, s0, w_in, conv_w, mu, w0, w2, a0, a2, g2, k_k, k_a, r_k,
             ln_gain, ln_bias, w_out):
    f32 = jnp.float32
    bsz, T, _ = h.shape
    p = h @ w_in
    pa, pb = p[..., :P_A], p[..., P_A:]
    gate_b, gate_c, xt = jnp.split(pa, 3, axis=-1)
    conv_out, new_conv = causal_dwconv(gate_c * xt, conv_buf, conv_w)
    y_a = gate_b * conv_out
    pb_prev, new_shift = token_shift(pb, shift_prev)
    pb = pb + mu * (pb_prev - pb)
    r = pb[..., :D_B]
    k = pb[..., D_B:2 * D_B]
    v = pb[..., 2 * D_B:3 * D_B]
    w_lo = pb[..., 3 * D_B:3 * D_B + R_W]
    a_lo = pb[..., 3 * D_B + R_W:3 * D_B + R_W + R_A]
    g_lo = pb[..., 3 * D_B + R_W + R_A:]
    w_log = -jax.nn.softplus(-(w0 + jnp.tanh(w_lo) @ w2).astype(f32)) - 0.5
    decay = jnp.exp(-jnp.exp(w_log))
    a = jax.nn.sigmoid((a0 + a_lo @ a2).astype(f32))
    g = jax.nn.sigmoid(g_lo.astype(f32)) @ g2.astype(f32)
    heads = lambda t: t.reshape(bsz, T, H_B, HD_B)
    kk = heads((k * k_k).astype(f32))
    kk = kk / jnp.maximum(jnp.sqrt(jnp.sum(kk * kk, axis=-1, keepdims=True)), 1e-12)
    k = k.astype(f32) * (1.0 + (a - 1.0) * k_a.astype(f32))
    r_h, k_h, v_h, a_h = heads(r.astype(f32)), heads(k), heads(v.astype(f32)), heads(a)
    y, s_final = wkv7_recurrence(r_h, heads(decay), k_h, v_h, kk, kk * a_h, s0)
    mean = jnp.mean(y, axis=-1, keepdims=True)
    var = jnp.mean(jnp.square(y - mean), axis=-1, keepdims=True)
    y = ((y - mean) * lax.rsqrt(var + GN_EPS_B)).reshape(bsz, T, D_B)
    y = y * ln_gain.astype(f32) + ln_bias.astype(f32)
    bonus = jnp.sum(r_h * k_h * r_k.astype(f32), axis=-1, keepdims=True) * v_h
    y_b = (y + bonus.reshape(bsz, T, D_B)) * g
    mixed = jnp.concatenate([y_a, y_b.astype(h.dtype)], axis=-1)
    return mixed @ w_out, new_conv, new_shift, s_final.astype(s0.dtype)


def diff_attn_prompt(q, k, v):
    f32 = jnp.float32
    bsz, S = q.shape[:2]
    nb = S // Q_BLOCK
    q_blocks = jnp.moveaxis(q.reshape(bsz, nb, Q_BLOCK, H_C, 2, HD_C), 1, 0)
    k32 = k.astype(f32)
    v32 = v.astype(f32)
    k_pos = jnp.arange(S)

    def block(args):
        i, qb = args
        s = jnp.einsum('bqhic,bkhic->bhiqk', qb.astype(f32) * ATTN_SCALE, k32)
        q_pos = i * Q_BLOCK + jnp.arange(Q_BLOCK)
        s = jnp.where(k_pos[None, :] <= q_pos[:, None], s, -jnp.inf)
        pr = jax.nn.softmax(s, axis=-1)
        return jnp.einsum('bhiqk,bkhd->bqhid', pr, v32)

    o = lax.map(block, (jnp.arange(nb), q_blocks))
    return jnp.moveaxis(o, 0, 1).reshape(bsz, S, H_C, 2, DV_C)


def online_softmax_update(carry, s, vals):
    m, l, acc = carry
    m_new = jnp.maximum(m, jnp.max(s, axis=-1))
    corr = jnp.exp(m - m_new)
    pr = jnp.exp(s - m_new[..., None])
    return (m_new, l * corr + jnp.sum(pr, axis=-1),
            acc * corr[..., None] + jnp.einsum('bhitp,bphd->bhitd', pr, vals))


def diff_attn_sample(q, k, v, cache_k, cache_v, page_table, layer):
    f32 = jnp.float32
    bsz, T = q.shape[:2]
    q32 = q.astype(f32) * ATTN_SCALE

    def page_step(carry, pids):
        kp = cache_k[layer, pids].astype(f32).reshape(bsz, PAGE_SIZE, H_C, 2, HD_C)
        vp = cache_v[layer, pids].astype(f32)
        s = jnp.einsum('bthic,bphic->bhitp', q32, kp)
        return online_softmax_update(carry, s, vp), None

    init = (jnp.full((bsz, H_C, 2, T), -jnp.inf, f32), jnp.zeros((bsz, H_C, 2, T), f32),
            jnp.zeros((bsz, H_C, 2, T, DV_C), f32))
    carry, _ = lax.scan(page_step, init, page_table.T)
    s = jnp.einsum('bthic,bshic->bhits', q32, k.astype(f32))
    causal = jnp.arange(T)[None, :] <= jnp.arange(T)[:, None]
    s = jnp.where(causal, s, -jnp.inf)
    _, l, acc = online_softmax_update(carry, s, v.astype(f32))
    return jnp.moveaxis(acc / l[..., None], 3, 1)


def mixer_c(h, cache_k, cache_v, page_table, layer, w_qkv, lq1, lk1, lq2, lk2, subln_w, w_out, lam_init):
    f32 = jnp.float32
    bsz, T, _ = h.shape
    qkv = h @ w_qkv
    q = qkv[..., :D_QK_C].reshape(bsz, T, H_C, 2, HD_C)
    k_rows = qkv[..., D_QK_C:2 * D_QK_C].reshape(bsz, T, 2 * H_C, HD_C)
    v_rows = qkv[..., 2 * D_QK_C:].reshape(bsz, T, H_C, DV_C)
    k = k_rows.reshape(bsz, T, H_C, 2, HD_C)
    if cache_k is None:
        o = diff_attn_prompt(q, k, v_rows)
    else:
        o = diff_attn_sample(q, k, v_rows, cache_k, cache_v, page_table, layer)
    lam = (jnp.exp(jnp.sum(lq1.astype(f32) * lk1.astype(f32)))
           - jnp.exp(jnp.sum(lq2.astype(f32) * lk2.astype(f32))) + lam_init)
    o = o[..., 0, :] - lam * o[..., 1, :]
    o = o * lax.rsqrt(jnp.mean(o * o, axis=-1, keepdims=True) + SUBLN_EPS) * subln_w.astype(f32)
    o = o * (1.0 - lam_init)
    return o.reshape(bsz, T, D_V_C).astype(h.dtype) @ w_out, k_rows, v_rows


def conv_ffn(h, buf, w_up, conv_w, conv_b, w_down):
    u = h @ w_up
    u, new_buf = causal_dwconv(u, buf, conv_w)
    u = u + conv_b
    gate, val = jnp.split(u, 2, axis=-1)
    return (jax.nn.silu(gate) * val) @ w_down, new_buf


def trunk(x, conv_a, shift_b, wkv_b, ffn_buf, cache_k, cache_v, page_table, P):
    new_conv, new_shift, new_wkv, new_k, new_v, new_ffn = [], [], [], [], [], []
    for l in range(DEPTH):
        j = l // 2
        h = rms_norm(x, P['norm_mix'][l])
        if l % 2 == 0:
            mix, c, s, st = mixer_ab(h, conv_a[j], shift_b[j], wkv_b[j], P['w_in_ab'][j], P['conv_w_a'][j],
                                     P['mu_b'][j], P['w0_b'][j], P['w2_b'][j], P['a0_b'][j], P['a2_b'][j],
                                     P['g2_b'][j], P['k_k_b'][j], P['k_a_b'][j], P['r_k_b'][j],
                                     P['ln_gain_b'][j], P['ln_bias_b'][j], P['w_out_ab'][j])
            new_conv.append(c)
            new_shift.append(s)
            new_wkv.append(st)
        else:
            lam_init = 0.8 - 0.6 * math.exp(-0.3 * l)
            mix, kr, vr = mixer_c(h, cache_k, cache_v, page_table, j, P['w_qkv_c'][j],
                                  P['lambda_q1_c'][j], P['lambda_k1_c'][j], P['lambda_q2_c'][j],
                                  P['lambda_k2_c'][j], P['subln_w_c'][j], P['w_out_c'][j], lam_init)
            new_k.append(kr)
            new_v.append(vr)
        x = x + mix
        h = rms_norm(x, P['norm_ffn'][l])
        f, fb = conv_ffn(h, ffn_buf[l], P['w_up_f'][l], P['conv_w_f'][l], P['conv_b_f'][l], P['w_down_f'][l])
        new_ffn.append(fb)
        x = x + f
    y = rms_norm(x, P['norm_final'])
    return (y, jnp.stack(new_conv), jnp.stack(new_shift), jnp.stack(new_wkv),
            jnp.stack(new_k), jnp.stack(new_v), jnp.stack(new_ffn))


def setup_inputs(seed: int = 0) -> dict:
    key = jax.random.key(seed)
    ks = iter(jax.random.split(key, 48))

    def nrm(shape, scale=1.0):
        return scale * jax.random.normal(next(ks), shape, jnp.float32)

    def unif(shape, lo, hi):
        return jax.random.uniform(next(ks), shape, jnp.float32, lo, hi)

    n_pages = PAST_LEN // PAGE_SIZE
    n_used = DEC_BATCH * n_pages
    n_phys = n_used + n_used // 4
    page_table = jax.random.permutation(next(ks), n_phys)[:n_used].reshape(DEC_BATCH, n_pages).astype(jnp.int32)
    return {
        'x_prompt': nrm((BATCH, SEQ, D_MODEL)),
        'x_sample': nrm((DEC_BATCH, DEC_SEQ, D_MODEL)),
        'state_conv_a': nrm((N_EVEN, DEC_BATCH, CONV_W - 1, D_A)),
        'state_shift_b': nrm((N_EVEN, DEC_BATCH, P_B)),
        'state_wkv_b': nrm((N_EVEN, DEC_BATCH, H_B, HD_B, HD_B), 0.5),
        'cache_k': nrm((N_ODD, n_phys, PAGE_SIZE, 2 * H_C, HD_C)),
        'cache_v': nrm((N_ODD, n_phys, PAGE_SIZE, H_C, DV_C)),
        'page_table': page_table,
        'state_ffn_conv': nrm((DEPTH, DEC_BATCH, CONV_W - 1, 2 * D_FF)),
        'norm_mix': 1.0 + nrm((DEPTH, D_MODEL), 0.01),
        'norm_ffn': 1.0 + nrm((DEPTH, D_MODEL), 0.01),
        'norm_final': 1.0 + nrm((D_MODEL,), 0.01),
        'w_in_ab': nrm((N_EVEN, D_MODEL, P_AB), D_MODEL ** -0.5),
        'conv_w_a': nrm((N_EVEN, CONV_W, D_A), CONV_W ** -0.5),
        'mu_b': unif((N_EVEN, P_B), 0.0, 1.0),
        'w0_b': unif((N_EVEN, D_B), -5.5, -0.5),
        'w2_b': nrm((N_EVEN, R_W, D_B), 0.5 * R_W ** -0.5),
        'a0_b': nrm((N_EVEN, D_B), 0.5),
        'a2_b': nrm((N_EVEN, R_A, D_B), 0.5 * R_A ** -0.5),
        'g2_b': nrm((N_EVEN, R_G, D_B), R_G ** -0.5),
        'k_k_b': 0.85 + nrm((N_EVEN, D_B), 0.02),
        'k_a_b': 1.0 + nrm((N_EVEN, D_B), 0.02),
        'r_k_b': nrm((N_EVEN, H_B, HD_B), 0.1),
        'ln_gain_b': 1.0 + nrm((N_EVEN, D_B), 0.02),
        'ln_bias_b': nrm((N_EVEN, D_B), 0.01),
        'w_out_ab': nrm((N_EVEN, D_A + D_B, D_MODEL), (D_A + D_B) ** -0.5),
        'w_qkv_c': nrm((N_ODD, D_MODEL, 2 * D_QK_C + D_V_C), D_MODEL ** -0.5),
        'lambda_q1_c': nrm((N_ODD, HD_C), 0.1),
        'lambda_k1_c': nrm((N_ODD, HD_C), 0.1),
        'lambda_q2_c': nrm((N_ODD, HD_C), 0.1),
        'lambda_k2_c': nrm((N_ODD, HD_C), 0.1),
        'subln_w_c': 1.0 + nrm((N_ODD, DV_C), 0.02),
        'w_out_c': nrm((N_ODD, D_V_C, D_MODEL), D_V_C ** -0.5),
        'w_up_f': nrm((DEPTH, D_MODEL, 2 * D_FF), D_MODEL ** -0.5),
        'conv_w_f': nrm((DEPTH, CONV_W, 2 * D_FF), CONV_W ** -0.5),
        'conv_b_f': nrm((DEPTH, 2 * D_FF), 0.01),
        'w_down_f': nrm((DEPTH, D_FF, D_MODEL), D_FF ** -0.5),
    }


def reference(x_prompt, x_sample, state_conv_a, state_shift_b, state_wkv_b, cache_k, cache_v, page_table,
              state_ffn_conv, norm_mix, norm_ffn, norm_final, w_in_ab, conv_w_a, mu_b, w0_b, w2_b, a0_b, a2_b,
              g2_b, k_k_b, k_a_b, r_k_b, ln_gain_b, ln_bias_b, w_out_ab, w_qkv_c, lambda_q1_c, lambda_k1_c,
              lambda_q2_c, lambda_k2_c, subln_w_c, w_out_c, w_up_f, conv_w_f, conv_b_f, w_down_f):
    P = dict(norm_mix=norm_mix, norm_ffn=norm_ffn, norm_final=norm_final, w_in_ab=w_in_ab, conv_w_a=conv_w_a,
             mu_b=mu_b, w0_b=w0_b, w2_b=w2_b, a0_b=a0_b, a2_b=a2_b, g2_b=g2_b, k_k_b=k_k_b, k_a_b=k_a_b,
             r_k_b=r_k_b, ln_gain_b=ln_gain_b, ln_bias_b=ln_bias_b, w_out_ab=w_out_ab, w_qkv_c=w_qkv_c,
             lambda_q1_c=lambda_q1_c, lambda_k1_c=lambda_k1_c, lambda_q2_c=lambda_q2_c,
             lambda_k2_c=lambda_k2_c, subln_w_c=subln_w_c, w_out_c=w_out_c, w_up_f=w_up_f,
             conv_w_f=conv_w_f, conv_b_f=conv_b_f, w_down_f=w_down_f)
    bp = x_prompt.shape[0]
    dt = x_prompt.dtype
    (y_prompt, p_conv_a, p_shift_b, p_wkv_b, p_k, p_v, p_ffn_conv) = trunk(
        x_prompt,
        jnp.zeros((N_EVEN, bp, CONV_W - 1, D_A), dt),
        jnp.zeros((N_EVEN, bp, P_B), dt),
        jnp.zeros((N_EVEN, bp, H_B, HD_B, HD_B), state_wkv_b.dtype),
        jnp.zeros((DEPTH, bp, CONV_W - 1, 2 * D_FF), dt),
        None, None, None, P)
    (y_sample, s_conv_a, s_shift_b, s_wkv_b, s_k, s_v, s_ffn_conv) = trunk(
        x_sample, state_conv_a, state_shift_b, state_wkv_b, state_ffn_conv,
        cache_k, cache_v, page_table, P)
    return (y_prompt, y_sample, p_conv_a, p_shift_b, p_wkv_b, p_k, p_v, p_ffn_conv,
            s_conv_a, s_shift_b, s_wkv_b, s_k, s_v, s_ffn_conv)
```

```python
import functools
import math

import jax
import jax.numpy as jnp
from jax import lax
from jax.experimental import pallas as pl
from jax.experimental.pallas import tpu as pltpu

F32 = jnp.float32
BF16 = jnp.bfloat16

D_MODEL = 2048
DEPTH = 2
PAGE_SIZE = 128
CONV_W = 3
NORM_EPS = 1e-6
D_A = D_MODEL // 2
P_A = 3 * D_A
D_B = D_MODEL // 2
HD_B = 64
H_B = D_B // HD_B
R_W = 64
R_A = 64
R_G = 160
P_B = 3 * D_B + R_W + R_A + R_G
LORA_PAD = 384
P_B_PAD = 3 * D_B + LORA_PAD
GN_EPS_B = 64e-5
HD_C = 128
H_C = D_MODEL // (2 * HD_C)
DV_C = 2 * HD_C
D_QK_C = 2 * H_C * HD_C
D_V_C = H_C * DV_C
ATTN_SCALE = HD_C ** -0.5
SUBLN_EPS = 1e-5
D_FF = 5632

LANES = 128
SUBLANES = 8
VMEM_LIMIT = 52 * 1024 * 1024
NEG_BIG = -1e30


def _params(sem):
    return pltpu.CompilerParams(dimension_semantics=sem, vmem_limit_bytes=VMEM_LIMIT)


def _rms_kernel(x_ref, g_ref, o_ref):
    x = x_ref[...]
    y = x * lax.rsqrt(jnp.mean(x * x, axis=-1, keepdims=True) + NORM_EPS)
    o_ref[...] = (y * g_ref[...]).astype(o_ref.dtype)


def _rms(x, g, out_dtype, tm=512):
    m, d = x.shape
    return pl.pallas_call(
        _rms_kernel,
        grid=(m // tm,),
        in_specs=[pl.BlockSpec((tm, d), lambda i: (i, 0)), pl.BlockSpec((1, d), lambda i: (0, 0))],
        out_specs=pl.BlockSpec((tm, d), lambda i: (i, 0)),
        out_shape=jax.ShapeDtypeStruct((m, d), out_dtype),
        compiler_params=_params(("parallel",)),
        name="rmsnorm",
    )(x, g.reshape(1, d))


def _mm_kernel(*refs, nx, has_res, scale):
    xs, ws = refs[:nx], refs[nx:2 * nx]
    res = refs[2 * nx] if has_res else None
    outs = refs[2 * nx + int(has_res):]
    acc = None
    for x, w in zip(xs, ws):
        d = jnp.dot(x[...].astype(BF16), w[...], preferred_element_type=F32)
        acc = d if acc is None else acc + d
    if scale is not None:
        acc = acc * scale
    if res is not None:
        acc = acc + res[...]
    for o in outs:
        o[...] = acc.astype(o.dtype)


def _mm(xs, ws, wblocks, n, *, tm, tn, out_dtypes=(F32,), res=None, scale=None, name="matmul"):
    m = xs[0].shape[0]
    in_specs, args = [], []
    for x in xs:
        in_specs.append(pl.BlockSpec((tm, x.shape[1]), lambda i, j: (i, 0)))
        args.append(x)
    for x, w, (rb, cb) in zip(xs, ws, wblocks):
        in_specs.append(pl.BlockSpec((x.shape[1], tn), lambda i, j, rb=rb, cb=cb: (rb, cb + j)))
        args.append(w)
    if res is not None:
        in_specs.append(pl.BlockSpec((tm, tn), lambda i, j: (i, j)))
        args.append(res)
    outs = pl.pallas_call(
        functools.partial(_mm_kernel, nx=len(xs), has_res=res is not None, scale=scale),
        grid=(m // tm, n // tn),
        in_specs=in_specs,
        out_specs=[pl.BlockSpec((tm, tn), lambda i, j: (i, j)) for _ in out_dtypes],
        out_shape=[jax.ShapeDtypeStruct((m, n), dt) for dt in out_dtypes],
        compiler_params=_params(("parallel", "arbitrary")),
        name=name,
    )(*args)
    return outs if len(out_dtypes) > 1 else outs[0]


def _seq_tiles(tm, seq_len):
    return (tm // seq_len, seq_len) if seq_len < tm else (1, tm)


def _shift_rows(u3, prev, k):
    t = lax.broadcasted_iota(jnp.int32, u3.shape, 1)
    out = pltpu.roll(u3, k, 1)
    for s in range(k):
        out = jnp.where(t == s, prev[:, 2 - k + s:3 - k + s, :], out)
    return out


def _conv3(u3, prev, w):
    return w[0:1, :] * _shift_rows(u3, prev, 2) + w[1:2, :] * _shift_rows(u3, prev, 1) + w[2:3, :] * u3


def _mixer_a_kernel(gb_ref, gc_ref, xt_ref, w_ref, buf_ref, ya_ref, nc_ref, carry_ref, *, nb, tt, carried):
    c = gb_ref.shape[-1]
    cx = (gc_ref[...] * xt_ref[...]).reshape(nb, tt, c)
    if carried:
        @pl.when(pl.program_id(0) == 0)
        def _():
            carry_ref[...] = buf_ref[...]
        prev = carry_ref[...]
    else:
        prev = buf_ref[...]
    conv = _conv3(cx, prev, w_ref[...])
    ya_ref[...] = (gb_ref[...] * conv.reshape(nb * tt, c)).astype(ya_ref.dtype)
    tail = cx[:, tt - 2:tt, :]
    nc_ref[...] = tail
    if carried:
        carry_ref[...] = tail


def _mixer_a(pa, conv_w, buf, seq_len, tm):
    m = pa.shape[0]
    nb, tt = _seq_tiles(tm, seq_len)
    carried = nb == 1
    bsz = buf.shape[0]
    buf_spec = (pl.BlockSpec((1, 2, D_A), lambda i: (0, 0, 0)) if carried
                else pl.BlockSpec((nb, 2, D_A), lambda i: (i, 0, 0)))
    ya, nc = pl.pallas_call(
        functools.partial(_mixer_a_kernel, nb=nb, tt=tt, carried=carried),
        grid=(m // tm,),
        in_specs=[pl.BlockSpec((tm, D_A), lambda i: (i, 0)),
                  pl.BlockSpec((tm, D_A), lambda i: (i, 1)),
                  pl.BlockSpec((tm, D_A), lambda i: (i, 2)),
                  pl.BlockSpec((CONV_W, D_A), lambda i: (0, 0)),
                  buf_spec],
        out_specs=[pl.BlockSpec((tm, D_A), lambda i: (i, 0)), buf_spec],
        out_shape=[jax.ShapeDtypeStruct((m, D_A), BF16), jax.ShapeDtypeStruct((bsz, 2, D_A), F32)],
        scratch_shapes=[pltpu.VMEM((1, 2, D_A), F32)],
        compiler_params=_params(("arbitrary",)),
        name="mixer_a",
    )(pa, pa, pa, conv_w, buf)
    return ya, nc


def _split_bf16(x):
    hi = x.astype(BF16)
    lo = (x - hi.astype(F32)).astype(BF16)
    return hi, lo


def _head_sum(x):
    blk = 2 * LANES
    r = lax.broadcasted_iota(jnp.int32, (blk, blk), 0) // HD_B
    c = lax.broadcasted_iota(jnp.int32, (blk, blk), 1) // HD_B
    ones = jnp.where(r == c, 1.0, 0.0).astype(BF16)
    hi, lo = _split_bf16(x)
    parts = []
    for s in range(x.shape[1] // blk):
        sl = slice(s * blk, (s + 1) * blk)
        parts.append(jnp.dot(hi[:, sl], ones, preferred_element_type=F32)
                     + jnp.dot(lo[:, sl], ones, preferred_element_type=F32))
    return jnp.concatenate(parts, axis=1)


def _prep_b_kernel(pb_ref, prev_ref, mu_ref, w0_ref, a0_ref, kk_ref, ka_ref, w2_ref, a2_ref, g2_ref,
                   r_out, q_out, w_out, k_out, v_out, kk_out, b_out, br_out, kr_out, g_out, ns_out,
                   carry_ref, *, nb, tt, carried):
    c = pb_ref.shape[-1]
    pb3 = pb_ref[...].reshape(nb, tt, c)
    if carried:
        @pl.when(pl.program_id(0) == 0)
        def _():
            carry_ref[...] = prev_ref[...]
        prev = carry_ref[...]
    else:
        prev = prev_ref[...]
    t = lax.broadcasted_iota(jnp.int32, pb3.shape, 1)
    shifted = jnp.where(t == 0, prev, pltpu.roll(pb3, 1, 1))
    last = pb3[:, tt - 1:tt, :]
    ns_out[...] = last
    if carried:
        carry_ref[...] = last
    x = (pb3 + mu_ref[...] * (shifted - pb3)).reshape(nb * tt, c)
    r = x[:, :D_B]
    k = x[:, D_B:2 * D_B]
    v = x[:, 2 * D_B:3 * D_B]
    lo = x[:, 3 * D_B:]
    wl = w0_ref[...] + jnp.dot(jnp.tanh(lo).astype(BF16), w2_ref[...], preferred_element_type=F32)
    softplus_neg = jnp.maximum(-wl, 0.0) + jnp.log(1.0 + jnp.exp(-jnp.abs(wl)))
    w_log = -softplus_neg - 0.5
    decay = jnp.exp(-jnp.exp(w_log))
    a = jax.nn.sigmoid(a0_ref[...] + jnp.dot(lo.astype(BF16), a2_ref[...], preferred_element_type=F32))
    g = jnp.dot(jax.nn.sigmoid(lo).astype(BF16), g2_ref[...], preferred_element_type=F32)
    kk = k * kk_ref[...]
    kk = kk / jnp.maximum(jnp.sqrt(_head_sum(kk * kk)), 1e-12)
    k2 = k * (1.0 + (a - 1.0) * ka_ref[...])
    b = kk * a
    r_out[...] = r
    q_out[...] = decay * r
    w_out[...] = decay
    k_out[...] = k2
    v_out[...] = v
    kk_out[...] = kk
    b_out[...] = b
    br_out[...] = _head_sum(b * r)
    kr_out[...] = _head_sum(k2 * r)
    g_out[...] = g


def _prep_b(pb, shift_prev, mu, w0, a0, k_k, k_a, w2p, a2p, g2p, seq_len, tm):
    m = pb.shape[0]
    nb, tt = _seq_tiles(tm, seq_len)
    carried = nb == 1
    bsz = shift_prev.shape[0]
    prev_spec = (pl.BlockSpec((1, 1, P_B_PAD), lambda i: (0, 0, 0)) if carried
                 else pl.BlockSpec((nb, 1, P_B_PAD), lambda i: (i, 0, 0)))
    row = lambda n: pl.BlockSpec((1, n), lambda i: (0, 0))
    lora = pl.BlockSpec((LORA_PAD, D_B), lambda i: (0, 0))
    tile = pl.BlockSpec((tm, D_B), lambda i: (i, 0))
    outs = pl.pallas_call(
        functools.partial(_prep_b_kernel, nb=nb, tt=tt, carried=carried),
        grid=(m // tm,),
        in_specs=[pl.BlockSpec((tm, P_B_PAD), lambda i: (i, 0)), prev_spec, row(P_B_PAD),
                  row(D_B), row(D_B), row(D_B), row(D_B), lora, lora, lora],
        out_specs=[tile] * 10 + [prev_spec],
        out_shape=[jax.ShapeDtypeStruct((m, D_B), F32)] * 10 + [jax.ShapeDtypeStruct((bsz, 1, P_B_PAD), F32)],
        scratch_shapes=[pltpu.VMEM((1, 1, P_B_PAD), F32)],
        compiler_params=_params(("arbitrary",)),
        name="rwkv_prep",
    )(pb, shift_prev, mu, w0, a0, k_k, k_a, w2p, a2p, g2p)
    return outs


def _wkv_kernel(q_ref, w_ref, k_ref, v_ref, kk_ref, b_ref, br_ref, kr_ref, s0_ref,
                yt_ref, sf_ref, s_scr, y_scr, *, tc):
    ci = pl.program_id(1)
    npair = H_B // 2

    @pl.when(ci == 0)
    def _():
        for j in range(npair):
            s_scr[j] = jnp.concatenate([s0_ref[0, 2 * j], s0_ref[0, 2 * j + 1]], axis=1)

    y_scr[...] = jnp.zeros_like(y_scr)
    lane = lax.broadcasted_iota(jnp.int32, (HD_B, LANES), 1)
    sub = lax.broadcasted_iota(jnp.int32, (HD_B, LANES), 0)
    first = lane < HD_B
    diag = (lane % HD_B) == sub
    tlane = lax.broadcasted_iota(jnp.int32, (1, LANES), 1)

    def pair_sums(x):
        sa = jnp.sum(jnp.where(first, x, 0.0), axis=1, keepdims=True)
        sb = jnp.sum(jnp.where(first, 0.0, x), axis=1, keepdims=True)
        return sa, sb

    def step(s, rows):
        kk_r, q_r, v_r, w_r, k_r, b_r, br, kr = rows
        z1a, z1b = pair_sums(s * kk_r)
        z2a, z2b = pair_sums(s * q_r)
        va, vb = pair_sums(jnp.where(diag, v_r, 0.0))
        z1 = jnp.where(first, z1a, z1b)
        vcol = jnp.where(first, va, vb)
        ya = z2a - z1a * br[:, 0:1] + va * kr[:, 0:1]
        yb = z2b - z1b * br[:, HD_B:HD_B + 1] + vb * kr[:, HD_B:HD_B + 1]
        return s * w_r + (vcol * k_r - z1 * b_r), ya, yb

    def group(t8, carry):
        base = pl.multiple_of(t8 * SUBLANES, SUBLANES)
        for j in range(npair):
            sl = slice(j * LANES, (j + 1) * LANES)
            rows = [ref[pl.ds(base, SUBLANES), sl]
                    for ref in (kk_ref, q_ref, v_ref, w_ref, k_ref, b_ref, br_ref, kr_ref)]
            s = s_scr[j]
            ya_acc = y_scr[2 * j]
            yb_acc = y_scr[2 * j + 1]
            for i in range(SUBLANES):
                s, ya, yb = step(s, [x[i:i + 1, :] for x in rows])
                onehot = jnp.where(tlane == base + i, 1.0, 0.0)
                ya_acc = ya_acc + ya * onehot
                yb_acc = yb_acc + yb * onehot
            s_scr[j] = s
            y_scr[2 * j] = ya_acc
            y_scr[2 * j + 1] = yb_acc
        return carry

    lax.fori_loop(0, tc // SUBLANES, group, 0)
    for h in range(H_B):
        yt_ref[0, h * HD_B:(h + 1) * HD_B, :] = y_scr[h][:, :tc]

    @pl.when(ci == pl.num_programs(1) - 1)
    def _():
        for j in range(npair):
            s = s_scr[j]
            sf_ref[0, 2 * j] = s[:, :HD_B]
            sf_ref[0, 2 * j + 1] = s[:, HD_B:]


def _wkv(q, w, k, v, kk, b, br, kr, s0, seq_len, tc):
    bsz = s0.shape[0]
    nc = seq_len // tc
    tile = pl.BlockSpec((tc, D_B), lambda bi, ci: (bi * nc + ci, 0))
    st = pl.BlockSpec((1, H_B, HD_B, HD_B), lambda bi, ci: (bi, 0, 0, 0))
    return pl.pallas_call(
        functools.partial(_wkv_kernel, tc=tc),
        grid=(bsz, nc),
        in_specs=[tile] * 8 + [st],
        out_specs=[pl.BlockSpec((1, D_B, tc), lambda bi, ci: (bi, 0, ci)), st],
        out_shape=[jax.ShapeDtypeStruct((bsz, D_B, seq_len), F32),
                   jax.ShapeDtypeStruct((bsz, H_B, HD_B, HD_B), F32)],
        scratch_shapes=[pltpu.VMEM((H_B // 2, HD_B, LANES), F32), pltpu.VMEM((H_B, HD_B, LANES), F32)],
        compiler_params=_params(("parallel", "arbitrary")),
        name="wkv_recurrence",
    )(q, w, k, v, kk, b, br, kr, s0)


def _post_b_kernel(y_ref, r_ref, k_ref, v_ref, g_ref, gain_ref, bias_ref, rk_ref, o_ref):
    y = y_ref[...]
    d = y - _head_sum(y) * (1.0 / HD_B)
    var = _head_sum(d * d) * (1.0 / HD_B)
    yn = d * lax.rsqrt(var + GN_EPS_B) * gain_ref[...] + bias_ref[...]
    v = v_ref[...]
    bonus = _head_sum(r_ref[...] * k_ref[...] * rk_ref[...]) * v
    o_ref[...] = ((yn + bonus) * g_ref[...]).astype(o_ref.dtype)


def _post_b(y, r, k, v, g, gain, bias, r_k, tm):
    m = y.shape[0]
    tile = pl.BlockSpec((tm, D_B), lambda i: (i, 0))
    row = pl.BlockSpec((1, D_B), lambda i: (0, 0))
    return pl.pallas_call(
        _post_b_kernel,
        grid=(m // tm,),
        in_specs=[tile] * 5 + [row] * 3,
        out_specs=tile,
        out_shape=jax.ShapeDtypeStruct((m, D_B), BF16),
        compiler_params=_params(("parallel",)),
        name="rwkv_post",
    )(y, r, k, v, g, gain, bias, r_k)


def _lambda(lam_ref, lam_init):
    l4 = lam_ref[...]
    s1 = jnp.sum(l4[0:1, :] * l4[1:2, :], axis=1, keepdims=True)
    s2 = jnp.sum(l4[2:3, :] * l4[3:4, :], axis=1, keepdims=True)
    return jnp.exp(s1) - jnp.exp(s2) + lam_init


def _diff_combine(o0, o1, lam, subw, lam_init):
    o = o0 - lam * o1
    o = o * lax.rsqrt(jnp.mean(o * o, axis=-1, keepdims=True) + SUBLN_EPS) * subw
    return o * (1.0 - lam_init)


def _flash_kernel(q_ref, k_ref, v_ref, lam_ref, subw_ref, o_ref, m_scr, l_scr, acc_scr, *, tq, tk, lam_init):
    qi, kj = pl.program_id(1), pl.program_id(2)

    @pl.when(kj == 0)
    def _():
        m_scr[...] = jnp.full_like(m_scr, NEG_BIG)
        l_scr[...] = jnp.zeros_like(l_scr)
        acc_scr[...] = jnp.zeros_like(acc_scr)

    def update(masked):
        v = v_ref[...]
        if masked:
            qpos = qi * tq + lax.broadcasted_iota(jnp.int32, (tq, tk), 0)
            kpos = kj * tk + lax.broadcasted_iota(jnp.int32, (tq, tk), 1)
            keep = kpos <= qpos
        for i in range(2):
            sl = slice(i * HD_C, (i + 1) * HD_C)
            s = lax.dot_general(q_ref[:, sl], k_ref[:, sl], (((1,), (1,)), ((), ())),
                                preferred_element_type=F32)
            if masked:
                s = jnp.where(keep, s, NEG_BIG)
            m_old = m_scr[i]
            m_new = jnp.maximum(m_old, jnp.max(s, axis=1, keepdims=True))
            corr = jnp.exp(m_old - m_new)
            p = jnp.exp(s - m_new)
            l_scr[i] = l_scr[i] * corr + jnp.sum(p, axis=1, keepdims=True)
            acc_scr[i] = acc_scr[i] * corr + jnp.dot(p.astype(BF16), v, preferred_element_type=F32)
            m_scr[i] = m_new

    first_k = kj * tk
    last_k = first_k + tk - 1

    @pl.when(last_k <= qi * tq)
    def _():
        update(False)

    @pl.when(jnp.logical_and(last_k > qi * tq, first_k <= qi * tq + tq - 1))
    def _():
        update(True)

    @pl.when(kj == pl.num_programs(2) - 1)
    def _():
        lam = _lambda(lam_ref, lam_init)
        o0 = acc_scr[0] / l_scr[0]
        o1 = acc_scr[1] / l_scr[1]
        o_ref[...] = _diff_combine(o0, o1, lam, subw_ref[...], lam_init).astype(o_ref.dtype)


def _flash_prompt(q, k, v, lam4, subw, lam_init, tq, tk):
    s = q.shape[0]
    nq, nk = s // tq, s // tk
    blk = 2 * HD_C

    def kv_map(h, i, j):
        return (jnp.minimum(j, (i * tq + tq - 1) // tk), h)

    return pl.pallas_call(
        functools.partial(_flash_kernel, tq=tq, tk=tk, lam_init=lam_init),
        grid=(H_C, nq, nk),
        in_specs=[pl.BlockSpec((tq, blk), lambda h, i, j: (i, h)),
                  pl.BlockSpec((tk, blk), kv_map),
                  pl.BlockSpec((tk, DV_C), kv_map),
                  pl.BlockSpec((4, HD_C), lambda h, i, j: (0, 0)),
                  pl.BlockSpec((1, DV_C), lambda h, i, j: (0, 0))],
        out_specs=pl.BlockSpec((tq, DV_C), lambda h, i, j: (i, h)),
        out_shape=jax.ShapeDtypeStruct((s, D_V_C), BF16),
        scratch_shapes=[pltpu.VMEM((2, tq, 1), F32), pltpu.VMEM((2, tq, 1), F32),
                        pltpu.VMEM((2, tq, DV_C), F32)],
        compiler_params=_params(("parallel", "parallel", "arbitrary")),
        name="diff_attn_prompt",
    )(q, k, v, lam4, subw)


def _decode_kernel(pt_ref, q_ref, kn_ref, vn_ref, kc_ref, vc_ref, lam_ref, subw_ref, o_ref,
                   qrow_scr, m_scr, l_scr, acc_scr, *, steps, lam_init):
    del pt_ref
    pj = pl.program_id(1)
    nrow = 2 * H_C * steps
    grp = 2 * steps

    @pl.when(pj == 0)
    def _():
        q = jnp.tile(q_ref[...], (2 * H_C, 1))
        r = lax.broadcasted_iota(jnp.int32, q.shape, 0) // steps
        c = lax.broadcasted_iota(jnp.int32, q.shape, 1) // HD_C
        qrow_scr[...] = jnp.where(r == c, q, 0.0).astype(BF16)
        m_scr[...] = jnp.full_like(m_scr, NEG_BIG)
        l_scr[...] = jnp.zeros_like(l_scr)
        acc_scr[...] = jnp.zeros_like(acc_scr)

    def update(k_bf, v_bf, keep):
        s = lax.dot_general(qrow_scr[...], k_bf, (((1,), (1,)), ((), ())), preferred_element_type=F32)
        if keep is not None:
            s = jnp.where(keep, s, NEG_BIG)
        m_old = m_scr[...]
        m_new = jnp.maximum(m_old, jnp.max(s, axis=1, keepdims=True))
        corr = jnp.exp(m_old - m_new)
        p = jnp.exp(s - m_new)
        l_scr[...] = l_scr[...] * corr + jnp.sum(p, axis=1, keepdims=True)
        m_scr[...] = m_new
        pb = p.astype(BF16)
        for h in range(H_C):
            rows = slice(h * grp, (h + 1) * grp)
            acc_scr[rows, :] = acc_scr[rows, :] * corr[rows, :] + jnp.dot(
                pb[rows, :], v_bf[:, h * DV_C:(h + 1) * DV_C], preferred_element_type=F32)

    update(kc_ref[...].astype(BF16), vc_ref[...].astype(BF16), None)

    @pl.when(pj == pl.num_programs(1) - 1)
    def _():
        pad = jnp.zeros((PAGE_SIZE - steps, kn_ref.shape[1]), F32)
        kn = jnp.concatenate([kn_ref[...], pad], axis=0).astype(BF16)
        vn = jnp.concatenate([vn_ref[...], pad], axis=0).astype(BF16)
        step = lax.broadcasted_iota(jnp.int32, (nrow, PAGE_SIZE), 0) % steps
        key = lax.broadcasted_iota(jnp.int32, (nrow, PAGE_SIZE), 1)
        update(kn, vn, key <= step)
        lam = _lambda(lam_ref, lam_init)
        o = acc_scr[...] / l_scr[...]
        for h in range(H_C):
            o0 = o[h * grp:h * grp + steps, :]
            o1 = o[h * grp + steps:(h + 1) * grp, :]
            o_ref[:, h * DV_C:(h + 1) * DV_C] = _diff_combine(o0, o1, lam, subw_ref[...], lam_init)


def _decode_attn(q, kn, vn, cache_k, cache_v, page_table, lam4, subw, lam_init, steps):
    bsz, n_pages = page_table.shape
    nrow = 2 * H_C * steps
    tile = lambda n: pl.BlockSpec((steps, n), lambda b, p, pt: (b, 0))
    page = lambda n: pl.BlockSpec((None, PAGE_SIZE, n), lambda b, p, pt: (pt[b * n_pages + p], 0, 0))
    grid_spec = pltpu.PrefetchScalarGridSpec(
        num_scalar_prefetch=1,
        grid=(bsz, n_pages),
        in_specs=[tile(D_QK_C), tile(D_QK_C), tile(D_V_C), page(D_QK_C), page(D_V_C),
                  pl.BlockSpec((4, HD_C), lambda b, p, pt: (0, 0)),
                  pl.BlockSpec((1, DV_C), lambda b, p, pt: (0, 0))],
        out_specs=tile(D_V_C),
        scratch_shapes=[pltpu.VMEM((nrow, D_QK_C), BF16), pltpu.VMEM((nrow, 1), F32),
                        pltpu.VMEM((nrow, 1), F32), pltpu.VMEM((nrow, DV_C), F32)],
    )
    return pl.pallas_call(
        functools.partial(_decode_kernel, steps=steps, lam_init=lam_init),
        grid_spec=grid_spec,
        out_shape=jax.ShapeDtypeStruct((bsz * steps, D_V_C), F32),
        compiler_params=_params(("parallel", "arbitrary")),
        name="diff_attn_decode",
    )(page_table.reshape(-1), q, kn, vn, cache_k, cache_v, lam4, subw)


def _ffn_up_kernel(h_ref, wg_ref, wv_ref, cwg_ref, cwv_ref, cbg_ref, cbv_ref, bg_ref, bv_ref,
                   o_ref, ng_ref, nv_ref, cg_scr, cv_scr, *, nb, tt, carried):
    c = wg_ref.shape[-1]
    h = h_ref[...]

    def branch(w_ref, cw_ref, cb_ref, buf_ref, new_ref, carry_ref):
        u = jnp.dot(h, w_ref[...], preferred_element_type=F32).reshape(nb, tt, c)
        if carried:
            @pl.when(pl.program_id(1) == 0)
            def _():
                carry_ref[...] = buf_ref[...]
            prev = carry_ref[...]
        else:
            prev = buf_ref[...]
        y = _conv3(u, prev, cw_ref[...]) + cb_ref[...]
        tail = u[:, tt - 2:tt, :]
        new_ref[...] = tail
        if carried:
            carry_ref[...] = tail
        return y.reshape(nb * tt, c)

    gate = branch(wg_ref, cwg_ref, cbg_ref, bg_ref, ng_ref, cg_scr)
    val = branch(wv_ref, cwv_ref, cbv_ref, bv_ref, nv_ref, cv_scr)
    o_ref[...] = (jax.nn.silu(gate) * val).astype(o_ref.dtype)


def _ffn_up(h, w_up, conv_w, conv_b, buf, seq_len, tm, tn):
    m, d = h.shape
    nj = D_FF // tn
    nb, tt = _seq_tiles(tm, seq_len)
    carried = nb == 1
    bsz = buf.shape[0]

    def cols(off):
        return lambda j, i: (0, j + off)

    def buf_spec(off):
        if carried:
            return pl.BlockSpec((1, 2, tn), lambda j, i: (0, 0, j + off))
        return pl.BlockSpec((nb, 2, tn), lambda j, i: (i, 0, j + off))

    return pl.pallas_call(
        functools.partial(_ffn_up_kernel, nb=nb, tt=tt, carried=carried),
        grid=(nj, m // tm),
        in_specs=[pl.BlockSpec((tm, d), lambda j, i: (i, 0)),
                  pl.BlockSpec((d, tn), cols(0)), pl.BlockSpec((d, tn), cols(nj)),
                  pl.BlockSpec((CONV_W, tn), cols(0)), pl.BlockSpec((CONV_W, tn), cols(nj)),
                  pl.BlockSpec((1, tn), cols(0)), pl.BlockSpec((1, tn), cols(nj)),
                  buf_spec(0), buf_spec(nj)],
        out_specs=[pl.BlockSpec((tm, tn), lambda j, i: (i, j)), buf_spec(0), buf_spec(0)],
        out_shape=[jax.ShapeDtypeStruct((m, D_FF), BF16),
                   jax.ShapeDtypeStruct((bsz, 2, D_FF), F32), jax.ShapeDtypeStruct((bsz, 2, D_FF), F32)],
        scratch_shapes=[pltpu.VMEM((1, 2, tn), F32), pltpu.VMEM((1, 2, tn), F32)],
        compiler_params=_params(("parallel", "arbitrary")),
        name="ffn_up",
    )(h, w_up, w_up, conv_w, conv_w, conv_b, conv_b, buf, buf)


def _pad_cols(x, n):
    return jnp.pad(x, [(0, 0)] * (x.ndim - 1) + [(0, n - x.shape[-1])])


def _trunk(x, conv_a, shift_b, wkv_b, ffn_buf, cache_k, cache_v, page_table, P, seq_len):
    m = x.shape[0]
    bsz = m // seq_len
    tm = 1024
    tm_small = 256
    tc = min(seq_len, LANES)

    h = _rms(x, P["norm_mix"][0], BF16)
    pa = _mm([h], [P["w_in_a"]], [(0, 0)], P_A, tm=tm, tn=1024, name="proj_in_a")
    pb = _mm([h], [P["w_in_b"]], [(0, 0)], P_B_PAD, tm=tm, tn=P_B_PAD // 3, name="proj_in_b")
    y_a, new_conv = _mixer_a(pa, P["conv_w_a"], conv_a, seq_len, 512)
    (r, q, w, k2, v, kk, b, br, kr, g, new_shift) = _prep_b(
        pb, shift_b, P["mu_b"], P["w0_b"], P["a0_b"], P["k_k_b"], P["k_a_b"],
        P["w2_b"], P["a2_b"], P["g2_b"], seq_len, tm_small)
    y_t, new_wkv = _wkv(q, w, k2, v, kk, b, br, kr, wkv_b, seq_len, tc)
    y = jnp.swapaxes(y_t, 1, 2).reshape(m, D_B)
    y_b = _post_b(y, r, k2, v, g, P["ln_gain_b"], P["ln_bias_b"], P["r_k_b"], tm_small)
    x = _mm([y_a, y_b], [P["w_out_ab"], P["w_out_ab"]], [(0, 0), (1, 0)], D_MODEL,
            tm=tm, tn=1024, res=x, name="proj_out_ab")

    h = _rms(x, P["norm_ffn"][0], BF16)
    gated, nfg0, nfv0 = _ffn_up(h, P["w_up_f"][0], P["conv_w_f"][0], P["conv_b_f"][0], ffn_buf[0],
                                seq_len, tm, 512)
    x = _mm([gated], [P["w_down_f"][0]], [(0, 0)], D_MODEL, tm=tm, tn=512, res=x, name="ffn_down")

    lam_init = 0.8 - 0.6 * math.exp(-0.3 * 1)
    h = _rms(x, P["norm_mix"][1], BF16)
    nq = D_QK_C // 1024
    if cache_k is None:
        qh = _mm([h], [P["w_qkv_c"]], [(0, 0)], D_QK_C, tm=tm, tn=1024, out_dtypes=(BF16,),
                 scale=ATTN_SCALE, name="proj_q")
        k_rows, k_bf = _mm([h], [P["w_qkv_c"]], [(0, nq)], D_QK_C, tm=tm, tn=1024,
                           out_dtypes=(F32, BF16), name="proj_k")
        v_rows, v_bf = _mm([h], [P["w_qkv_c"]], [(0, 2 * nq)], D_V_C, tm=tm, tn=1024,
                           out_dtypes=(F32, BF16), name="proj_v")
        attn = _flash_prompt(qh, k_bf, v_bf, P["lam4"], P["subln_w_c"], lam_init, 512, 512)
    else:
        qh = _mm([h], [P["w_qkv_c"]], [(0, 0)], D_QK_C, tm=tm, tn=1024, scale=ATTN_SCALE, name="proj_q")
        k_rows = _mm([h], [P["w_qkv_c"]], [(0, nq)], D_QK_C, tm=tm, tn=1024, name="proj_k")
        v_rows = _mm([h], [P["w_qkv_c"]], [(0, 2 * nq)], D_V_C, tm=tm, tn=1024, name="proj_v")
        attn = _decode_attn(qh, k_rows, v_rows, cache_k, cache_v, page_table, P["lam4"], P["subln_w_c"],
                            lam_init, seq_len)
    x = _mm([attn], [P["w_out_c"]], [(0, 0)], D_MODEL, tm=tm, tn=1024, res=x, name="proj_out_c")

    h = _rms(x, P["norm_ffn"][1], BF16)
    gated, nfg1, nfv1 = _ffn_up(h, P["w_up_f"][1], P["conv_w_f"][1], P["conv_b_f"][1], ffn_buf[1],
                                seq_len, tm, 512)
    x = _mm([gated], [P["w_down_f"][1]], [(0, 0)], D_MODEL, tm=tm, tn=512, res=x, name="ffn_down")

    y_out = _rms(x, P["norm_final"], F32)
    new_ffn = jnp.stack([jnp.concatenate([nfg0, nfv0], axis=-1), jnp.concatenate([nfg1, nfv1], axis=-1)])
    return (y_out.reshape(bsz, seq_len, D_MODEL),
            new_conv[None],
            new_shift[None, :, 0, :P_B],
            new_wkv[None],
            k_rows.reshape(1, bsz, seq_len, 2 * H_C, HD_C),
            v_rows.reshape(1, bsz, seq_len, H_C, DV_C),
            new_ffn)


def kernel(x_prompt, x_sample, state_conv_a, state_shift_b, state_wkv_b, cache_k, cache_v, page_table,
           state_ffn_conv, norm_mix, norm_ffn, norm_final, w_in_ab, conv_w_a, mu_b, w0_b, w2_b, a0_b, a2_b,
           g2_b, k_k_b, k_a_b, r_k_b, ln_gain_b, ln_bias_b, w_out_ab, w_qkv_c, lambda_q1_c, lambda_k1_c,
           lambda_q2_c, lambda_k2_c, subln_w_c, w_out_c, w_up_f, conv_w_f, conv_b_f, w_down_f):
    bp, seq, d = x_prompt.shape
    bs, dec_seq, _ = x_sample.shape
    row = lambda a: a.reshape(1, -1)
    lora_rows = jnp.zeros((LORA_PAD, D_B), F32)
    P = dict(
        norm_mix=norm_mix, norm_ffn=norm_ffn, norm_final=norm_final,
        w_in_a=w_in_ab[0, :, :P_A].astype(BF16),
        w_in_b=_pad_cols(w_in_ab[0, :, P_A:], P_B_PAD).astype(BF16),
        conv_w_a=conv_w_a[0],
        mu_b=_pad_cols(row(mu_b[0]), P_B_PAD),
        w0_b=row(w0_b[0]), a0_b=row(a0_b[0]), k_k_b=row(k_k_b[0]), k_a_b=row(k_a_b[0]),
        w2_b=lora_rows.at[:R_W].set(w2_b[0]).astype(BF16),
        a2_b=lora_rows.at[R_W:R_W + R_A].set(a2_b[0]).astype(BF16),
        g2_b=lora_rows.at[R_W + R_A:R_W + R_A + R_G].set(g2_b[0]).astype(BF16),
        r_k_b=row(r_k_b[0]), ln_gain_b=row(ln_gain_b[0]), ln_bias_b=row(ln_bias_b[0]),
        w_out_ab=w_out_ab[0].astype(BF16),
        w_qkv_c=w_qkv_c[0].astype(BF16),
        lam4=jnp.stack([lambda_q1_c[0], lambda_k1_c[0], lambda_q2_c[0], lambda_k2_c[0]]),
        subln_w_c=row(subln_w_c[0]),
        w_out_c=w_out_c[0].astype(BF16),
        w_up_f=w_up_f.astype(BF16), conv_w_f=conv_w_f, conv_b_f=conv_b_f[:, None, :],
        w_down_f=w_down_f.astype(BF16),
    )
    n_phys = cache_k.shape[1]
    outs_p = _trunk(
        x_prompt.reshape(bp * seq, d),
        jnp.zeros((bp, CONV_W - 1, D_A), F32),
        jnp.zeros((bp, 1, P_B_PAD), F32),
        jnp.zeros((bp, H_B, HD_B, HD_B), F32),
        jnp.zeros((DEPTH, bp, CONV_W - 1, 2 * D_FF), F32),
        None, None, None, P, seq)
    outs_s = _trunk(
        x_sample.reshape(bs * dec_seq, d),
        state_conv_a[0],
        _pad_cols(state_shift_b[0], P_B_PAD)[:, None, :],
        state_wkv_b[0],
        state_ffn_conv,
        cache_k[0].reshape(n_phys, PAGE_SIZE, D_QK_C),
        cache_v[0].reshape(n_phys, PAGE_SIZE, D_V_C),
        page_table, P, dec_seq)
    return (outs_p[0], outs_s[0]) + outs_p[1:] + outs_s[1:]
```

```python
import functools
import math

import jax
import jax.numpy as jnp
from jax import lax
from jax.experimental import pallas as pl
from jax.experimental.pallas import tpu as pltpu

F32 = jnp.float32
BF16 = jnp.bfloat16

D_MODEL = 2048
DEPTH = 2
PAGE_SIZE = 128
CONV_W = 3
NORM_EPS = 1e-6
D_A = D_MODEL // 2
P_A = 3 * D_A
D_B = D_MODEL // 2
HD_B = 64
H_B = D_B // HD_B
R_W = 64
R_A = 64
R_G = 160
P_B = 3 * D_B + R_W + R_A + R_G
LORA_PAD = 384
P_B_PAD = 3 * D_B + LORA_PAD
GN_EPS_B = 64e-5
HD_C = 128
H_C = D_MODEL // (2 * HD_C)
DV_C = 2 * HD_C
D_QK_C = 2 * H_C * HD_C
D_V_C = H_C * DV_C
ATTN_SCALE = HD_C ** -0.5
SUBLN_EPS = 1e-5
D_FF = 5632

LANES = 128
SUBLANES = 8
VMEM_LIMIT = 52 * 1024 * 1024
NEG_BIG = -1e30


def _params(sem):
    return pltpu.CompilerParams(dimension_semantics=sem, vmem_limit_bytes=VMEM_LIMIT)


def _rms_kernel(x_ref, g_ref, o_ref):
    x = x_ref[...]
    y = x * lax.rsqrt(jnp.mean(x * x, axis=-1, keepdims=True) + NORM_EPS)
    o_ref[...] = (y * g_ref[...]).astype(o_ref.dtype)


def _rms(x, g, out_dtype, tm=512):
    m, d = x.shape
    return pl.pallas_call(
        _rms_kernel,
        grid=(m // tm,),
        in_specs=[pl.BlockSpec((tm, d), lambda i: (i, 0)), pl.BlockSpec((1, d), lambda i: (0, 0))],
        out_specs=pl.BlockSpec((tm, d), lambda i: (i, 0)),
        out_shape=jax.ShapeDtypeStruct((m, d), out_dtype),
        compiler_params=_params(("parallel",)),
        name="rmsnorm",
    )(x, g.reshape(1, d))


def _mm_kernel(*refs, nx, has_res, scale):
    xs, ws = refs[:nx], refs[nx:2 * nx]
    res = refs[2 * nx] if has_res else None
    outs = refs[2 * nx + int(has_res):]
    acc = None
    for x, w in zip(xs, ws):
        d = jnp.dot(x[...].astype(BF16), w[...], preferred_element_type=F32)
        acc = d if acc is None else acc + d
    if scale is not None:
        acc = acc * scale
    if res is not None:
        acc = acc + res[...]
    for o in outs:
        o[...] = acc.astype(o.dtype)


def _mm(xs, ws, wblocks, n, *, tm, tn, out_dtypes=(F32,), res=None, scale=None, name="matmul"):
    m = xs[0].shape[0]
    in_specs, args = [], []
    for x in xs:
        in_specs.append(pl.BlockSpec((tm, x.shape[1]), lambda i, j: (i, 0)))
        args.append(x)
    for x, w, (rb, cb) in zip(xs, ws, wblocks):
        in_specs.append(pl.BlockSpec((x.shape[1], tn), lambda i, j, rb=rb, cb=cb: (rb, cb + j)))
        args.append(w)
    if res is not None:
        in_specs.append(pl.BlockSpec((tm, tn), lambda i, j: (i, j)))
        args.append(res)
    outs = pl.pallas_call(
        functools.partial(_mm_kernel, nx=len(xs), has_res=res is not None, scale=scale),
        grid=(m // tm, n // tn),
        in_specs=in_specs,
        out_specs=[pl.BlockSpec((tm, tn), lambda i, j: (i, j)) for _ in out_dtypes],
        out_shape=[jax.ShapeDtypeStruct((m, n), dt) for dt in out_dtypes],
        compiler_params=_params(("parallel", "arbitrary")),
        name=name,
    )(*args)
    return outs if len(out_dtypes) > 1 else outs[0]


def _seq_tiles(tm, seq_len):
    return (tm // seq_len, seq_len) if seq_len < tm else (1, tm)


def _shift_rows(u3, prev, k):
    t = lax.broadcasted_iota(jnp.int32, u3.shape, 1)
    out = pltpu.roll(u3, k, 1)
    for s in range(k):
        out = jnp.where(t == s, prev[:, 2 - k + s:3 - k + s, :], out)
    return out


def _conv3(u3, prev, w):
    return w[0:1, :] * _shift_rows(u3, prev, 2) + w[1:2, :] * _shift_rows(u3, prev, 1) + w[2:3, :] * u3


def _mixer_a_kernel(gb_ref, gc_ref, xt_ref, w_ref, buf_ref, ya_ref, nc_ref, carry_ref, *, nb, tt, carried):
    c = gb_ref.shape[-1]
    cx = (gc_ref[...] * xt_ref[...]).reshape(nb, tt, c)
    if carried:
        @pl.when(pl.program_id(0) == 0)
        def _():
            carry_ref[...] = buf_ref[...]
        prev = carry_ref[...]
    else:
        prev = buf_ref[...]
    conv = _conv3(cx, prev, w_ref[...])
    ya_ref[...] = (gb_ref[...] * conv.reshape(nb * tt, c)).astype(ya_ref.dtype)
    tail = cx[:, tt - 2:tt, :]
    nc_ref[...] = tail
    if carried:
        carry_ref[...] = tail


def _mixer_a(pa, conv_w, buf, seq_len, tm):
    m = pa.shape[0]
    nb, tt = _seq_tiles(tm, seq_len)
    carried = nb == 1
    bsz = buf.shape[0]
    buf_spec = (pl.BlockSpec((1, 2, D_A), lambda i: (0, 0, 0)) if carried
                else pl.BlockSpec((nb, 2, D_A), lambda i: (i, 0, 0)))
    ya, nc = pl.pallas_call(
        functools.partial(_mixer_a_kernel, nb=nb, tt=tt, carried=carried),
        grid=(m // tm,),
        in_specs=[pl.BlockSpec((tm, D_A), lambda i: (i, 0)),
                  pl.BlockSpec((tm, D_A), lambda i: (i, 1)),
                  pl.BlockSpec((tm, D_A), lambda i: (i, 2)),
                  pl.BlockSpec((CONV_W, D_A), lambda i: (0, 0)),
                  buf_spec],
        out_specs=[pl.BlockSpec((tm, D_A), lambda i: (i, 0)), buf_spec],
        out_shape=[jax.ShapeDtypeStruct((m, D_A), BF16), jax.ShapeDtypeStruct((bsz, 2, D_A), F32)],
        scratch_shapes=[pltpu.VMEM((1, 2, D_A), F32)],
        compiler_params=_params(("arbitrary",)),
        name="mixer_a",
    )(pa, pa, pa, conv_w, buf)
    return ya, nc


def _split_bf16(x):
    hi = x.astype(BF16)
    lo = (x - hi.astype(F32)).astype(BF16)
    return hi, lo


def _head_sum(x):
    blk = 2 * LANES
    r = lax.broadcasted_iota(jnp.int32, (blk, blk), 0) // HD_B
    c = lax.broadcasted_iota(jnp.int32, (blk, blk), 1) // HD_B
    ones = jnp.where(r == c, 1.0, 0.0).astype(BF16)
    hi, lo = _split_bf16(x)
    parts = []
    for s in range(x.shape[1] // blk):
        sl = slice(s * blk, (s + 1) * blk)
        parts.append(jnp.dot(hi[:, sl], ones, preferred_element_type=F32)
                     + jnp.dot(lo[:, sl], ones, preferred_element_type=F32))
    return jnp.concatenate(parts, axis=1)


def _prep_b_kernel(pb_ref, prev_ref, mu_ref, w0_ref, a0_ref, kk_ref, ka_ref, w2_ref, a2_ref, g2_ref,
                   r_out, lw_out, k_out, v_out, kk_out, b_out, g_out, ns_out,
                   carry_ref, *, nb, tt, carried):
    c = pb_ref.shape[-1]
    pb3 = pb_ref[...].reshape(nb, tt, c)
    if carried:
        @pl.when(pl.program_id(0) == 0)
        def _():
            carry_ref[...] = prev_ref[...]
        prev = carry_ref[...]
    else:
        prev = prev_ref[...]
    t = lax.broadcasted_iota(jnp.int32, pb3.shape, 1)
    shifted = jnp.where(t == 0, prev, pltpu.roll(pb3, 1, 1))
    last = pb3[:, tt - 1:tt, :]
    ns_out[...] = last
    if carried:
        carry_ref[...] = last
    x = (pb3 + mu_ref[...] * (shifted - pb3)).reshape(nb * tt, c)
    r = x[:, :D_B]
    k = x[:, D_B:2 * D_B]
    v = x[:, 2 * D_B:3 * D_B]
    lo = x[:, 3 * D_B:]
    wl = w0_ref[...] + jnp.dot(jnp.tanh(lo).astype(BF16), w2_ref[...], preferred_element_type=F32)
    softplus_neg = jnp.maximum(-wl, 0.0) + jnp.log(1.0 + jnp.exp(-jnp.abs(wl)))
    w_log = -softplus_neg - 0.5
    a = jax.nn.sigmoid(a0_ref[...] + jnp.dot(lo.astype(BF16), a2_ref[...], preferred_element_type=F32))
    g = jnp.dot(jax.nn.sigmoid(lo).astype(BF16), g2_ref[...], preferred_element_type=F32)
    kk = k * kk_ref[...]
    kk = kk / jnp.maximum(jnp.sqrt(_head_sum(kk * kk)), 1e-12)
    k2 = k * (1.0 + (a - 1.0) * ka_ref[...])
    b = kk * a
    r_out[...] = r
    lw_out[...] = -jnp.exp(w_log)
    k_out[...] = k2
    v_out[...] = v
    kk_out[...] = kk
    b_out[...] = b
    g_out[...] = g


def _prep_b(pb, shift_prev, mu, w0, a0, k_k, k_a, w2p, a2p, g2p, seq_len, tm):
    m = pb.shape[0]
    nb, tt = _seq_tiles(tm, seq_len)
    carried = nb == 1
    bsz = shift_prev.shape[0]
    prev_spec = (pl.BlockSpec((1, 1, P_B_PAD), lambda i: (0, 0, 0)) if carried
                 else pl.BlockSpec((nb, 1, P_B_PAD), lambda i: (i, 0, 0)))
    row = lambda n: pl.BlockSpec((1, n), lambda i: (0, 0))
    lora = pl.BlockSpec((LORA_PAD, D_B), lambda i: (0, 0))
    tile = pl.BlockSpec((tm, D_B), lambda i: (i, 0))
    outs = pl.pallas_call(
        functools.partial(_prep_b_kernel, nb=nb, tt=tt, carried=carried),
        grid=(m // tm,),
        in_specs=[pl.BlockSpec((tm, P_B_PAD), lambda i: (i, 0)), prev_spec, row(P_B_PAD),
                  row(D_B), row(D_B), row(D_B), row(D_B), lora, lora, lora],
        out_specs=[tile] * 7 + [prev_spec],
        out_shape=[jax.ShapeDtypeStruct((m, D_B), F32)] * 7 + [jax.ShapeDtypeStruct((bsz, 1, P_B_PAD), F32)],
        scratch_shapes=[pltpu.VMEM((1, 1, P_B_PAD), F32)],
        compiler_params=_params(("arbitrary",)),
        name="rwkv_prep",
    )(pb, shift_prev, mu, w0, a0, k_k, k_a, w2p, a2p, g2p)
    return outs


def _wkv_kernel(r_ref, lw_ref, k_ref, v_ref, kk_ref, b_ref, s0_ref, yt_ref, sf_ref, s_scr, y_scr, *, tc):
    ci = pl.program_id(1)
    npair = H_B // 2

    @pl.when(ci == 0)
    def _():
        for j in range(npair):
            s_scr[j] = jnp.concatenate([s0_ref[0, 2 * j], s0_ref[0, 2 * j + 1]], axis=1)

    y_scr[...] = jnp.zeros_like(y_scr)
    lane = lax.broadcasted_iota(jnp.int32, (HD_B, LANES), 1)
    sub = lax.broadcasted_iota(jnp.int32, (HD_B, LANES), 0)
    first = lane < HD_B
    diag = (lane % HD_B) == sub
    tlane = lax.broadcasted_iota(jnp.int32, (1, LANES), 1)

    def pair_sums(x):
        mine = first[:x.shape[0]]
        sa = jnp.sum(jnp.where(mine, x, 0.0), axis=1, keepdims=True)
        sb = jnp.sum(jnp.where(mine, 0.0, x), axis=1, keepdims=True)
        return sa, sb

    def step(s, rows):
        kk_r, q_r, v_r, w_r, k_r, b_r, br_a, br_b, kr_a, kr_b = rows
        z1a, z1b = pair_sums(s * kk_r)
        z2a, z2b = pair_sums(s * q_r)
        va, vb = pair_sums(jnp.where(diag, v_r, 0.0))
        z1 = jnp.where(first, z1a, z1b)
        vcol = jnp.where(first, va, vb)
        ya = z2a - z1a * br_a + va * kr_a
        yb = z2b - z1b * br_b + vb * kr_b
        return s * w_r + (vcol * k_r - z1 * b_r), ya, yb

    def group(t8, carry):
        base = pl.multiple_of(t8 * SUBLANES, SUBLANES)
        tiles = []
        for j in range(npair):
            sl = slice(j * LANES, (j + 1) * LANES)
            r8, lw8, k8, v8, kk8, b8 = [ref[pl.ds(base, SUBLANES), sl]
                                        for ref in (r_ref, lw_ref, k_ref, v_ref, kk_ref, b_ref)]
            w8 = jnp.exp(lw8)
            tiles.append((kk8, w8 * r8, v8, w8, k8, b8) + pair_sums(b8 * r8) + pair_sums(k8 * r8))
        for i in range(SUBLANES):
            onehot = jnp.where(tlane == base + i, 1.0, 0.0)
            for j in range(npair):
                s, ya, yb = step(s_scr[j], [x[i:i + 1, :] for x in tiles[j]])
                s_scr[j] = s
                y_scr[2 * j] += ya * onehot
                y_scr[2 * j + 1] += yb * onehot
        return carry

    lax.fori_loop(0, tc // SUBLANES, group, 0)
    for h in range(H_B):
        yt_ref[0, h * HD_B:(h + 1) * HD_B, :] = y_scr[h][:, :tc]

    @pl.when(ci == pl.num_programs(1) - 1)
    def _():
        for j in range(npair):
            s = s_scr[j]
            sf_ref[0, 2 * j] = s[:, :HD_B]
            sf_ref[0, 2 * j + 1] = s[:, HD_B:]


def _wkv(r, lw, k, v, kk, b, s0, seq_len, tc):
    bsz = s0.shape[0]
    nc = seq_len // tc
    tile = pl.BlockSpec((tc, D_B), lambda bi, ci: (bi * nc + ci, 0))
    st = pl.BlockSpec((1, H_B, HD_B, HD_B), lambda bi, ci: (bi, 0, 0, 0))
    return pl.pallas_call(
        functools.partial(_wkv_kernel, tc=tc),
        grid=(bsz, nc),
        in_specs=[tile] * 6 + [st],
        out_specs=[pl.BlockSpec((1, D_B, tc), lambda bi, ci: (bi, 0, ci)), st],
        out_shape=[jax.ShapeDtypeStruct((bsz, D_B, seq_len), F32),
                   jax.ShapeDtypeStruct((bsz, H_B, HD_B, HD_B), F32)],
        scratch_shapes=[pltpu.VMEM((H_B // 2, HD_B, LANES), F32), pltpu.VMEM((H_B, HD_B, LANES), F32)],
        compiler_params=_params(("parallel", "arbitrary")),
        name="wkv_recurrence",
    )(r, lw, k, v, kk, b, s0)


def _split3_bf16(x):
    hi = x.astype(BF16)
    rest = x - hi.astype(F32)
    mid = rest.astype(BF16)
    return hi, mid, (rest - mid.astype(F32)).astype(BF16)


def _dot_bf16(a, b, dims=(((1,), (0,)), ((), ()))):
    return lax.dot_general(a.astype(BF16), b.astype(BF16), dims, preferred_element_type=F32)


_NT = (((1,), (1,)), ((), ()))
_TN = (((0,), (0,)), ((), ()))


def _wkv_chunk_kernel(r_ref, lw_ref, k_ref, v_ref, kk_ref, b_ref, s0_ref, y_ref, sf_ref, s_scr, *, c):
    ci = pl.program_id(1)
    npair = H_B // 2

    @pl.when(ci == 0)
    def _():
        s_scr[...] = jnp.zeros_like(s_scr)
        for j in range(npair):
            s_scr[j, 0:HD_B, 0:HD_B] = s0_ref[0, 2 * j]
            s_scr[j, HD_B:, HD_B:] = s0_ref[0, 2 * j + 1]

    ti = lax.broadcasted_iota(jnp.int32, (c, c), 0)
    tj = lax.broadcasted_iota(jnp.int32, (c, c), 1)
    tri = jnp.where(tj <= ti, 1.0, 0.0).astype(BF16)
    lw = lw_ref[...]
    cum = None
    for limb in _split3_bf16(lw):
        d = jnp.dot(tri, limb, preferred_element_type=F32)
        cum = d if cum is None else cum + d
    last = cum[c - 1:c, :]
    mid = cum[c // 2 - 1:c // 2, :]
    cc = cum - mid
    g_mid = jnp.exp(mid)
    g_inv = jnp.exp(-cc)
    kt = kk_ref[...] * jnp.exp(cc - lw)
    rt = r_ref[...] * jnp.exp(cc)
    bt = b_ref[...] * g_inv
    kq = k_ref[...] * g_inv
    to_end = jnp.exp(last - cum)
    kh = k_ref[...] * to_end
    bh = b_ref[...] * to_end
    g_last = jnp.exp(last)
    v = v_ref[...]

    first = lax.broadcasted_iota(jnp.int32, (c, LANES), 1) < HD_B
    col2 = lax.broadcasted_iota(jnp.int32, (c, 2 * c), 1)
    row2 = lax.broadcasted_iota(jnp.int32, (c, 2 * c), 0)
    strict = (col2 % c) < row2
    incl = (col2 % c) <= row2
    second_grp = col2 >= c
    strict_sq = tj < ti
    blockdiag = (lax.broadcasted_iota(jnp.int32, (LANES, LANES), 0) // HD_B
                 == lax.broadcasted_iota(jnp.int32, (LANES, LANES), 1) // HD_B)
    lane_l = lax.broadcasted_iota(jnp.int32, (2 * c, LANES), 1)
    nsq = (c - 1).bit_length() - 1

    sls = [slice(j * LANES, (j + 1) * LANES) for j in range(npair)]
    s2s = [s_scr[j] for j in range(npair)]
    lhs2s = [jnp.concatenate([kt[:, sl], rt[:, sl]], axis=0) for sl in sls]
    rhs2s = [jnp.concatenate([bt[:, sl], kq[:, sl]], axis=0) for sl in sls]
    vps = [v[:, sl] for sl in sls]
    wy0s = [_dot_bf16(lhs2s[j], s2s[j] * g_mid[:, sls[j]], _NT) for j in range(npair)]
    heads = [(j, hh) for j in range(npair) for hh in range(2)]
    ns, rs, xs = [], [], []
    for j, hh in heads:
        mine = (lane_l < HD_B) if hh == 0 else (lane_l >= HD_B)
        m1 = _dot_bf16(jnp.where(mine, lhs2s[j], 0.0), rhs2s[j], _NT)
        ab_ak = jnp.where(strict, m1[:c], 0.0)
        rs.append(jnp.where(incl, m1[c:], 0.0))
        ns.append(jnp.where(strict_sq, m1[:c, :c], 0.0))
        xs.append(wy0s[j][:c] + _dot_bf16(jnp.where(second_grp, ab_ak, 0.0),
                                          jnp.concatenate([vps[j], vps[j]], axis=0)))
    xs = [x - _dot_bf16(n, x) for n, x in zip(ns, xs)]
    ps = ns
    for _ in range(nsq):
        ps = [_dot_bf16(p, p) for p in ps]
        xs = [x + _dot_bf16(p, x) for p, x in zip(ps, xs)]
    for j in range(npair):
        sl = sls[j]
        u = jnp.where(first, xs[2 * j], xs[2 * j + 1])
        stack = jnp.concatenate([-u, vps[j]], axis=0)
        y_ref[:, sl] = wy0s[j][c:] + jnp.where(first, _dot_bf16(rs[2 * j], stack),
                                               _dot_bf16(rs[2 * j + 1], stack))
        ds = _dot_bf16(jnp.concatenate([vps[j], -u], axis=0),
                       jnp.concatenate([kh[:, sl], bh[:, sl]], axis=0), _TN)
        s_scr[j] = s2s[j] * g_last[:, sl] + jnp.where(blockdiag, ds, 0.0)

    @pl.when(ci == pl.num_programs(1) - 1)
    def _():
        for j in range(npair):
            s2 = s_scr[j]
            sf_ref[0, 2 * j] = s2[0:HD_B, 0:HD_B]
            sf_ref[0, 2 * j + 1] = s2[HD_B:, HD_B:]


def _wkv_chunked(r, lw, k, v, kk, b, s0, seq_len, c):
    bsz = s0.shape[0]
    nc = seq_len // c
    tile = pl.BlockSpec((c, D_B), lambda bi, ci: (bi * nc + ci, 0))
    st = pl.BlockSpec((1, H_B, HD_B, HD_B), lambda bi, ci: (bi, 0, 0, 0))
    return pl.pallas_call(
        functools.partial(_wkv_chunk_kernel, c=c),
        grid=(bsz, nc),
        in_specs=[tile] * 6 + [st],
        out_specs=[tile, st],
        out_shape=[jax.ShapeDtypeStruct((bsz * seq_len, D_B), F32),
                   jax.ShapeDtypeStruct((bsz, H_B, HD_B, HD_B), F32)],
        scratch_shapes=[pltpu.VMEM((H_B // 2, LANES, LANES), F32)],
        compiler_params=_params(("parallel", "arbitrary")),
        name="wkv_chunked",
    )(r, lw, k, v, kk, b, s0)


def _post_b_kernel(y_ref, r_ref, k_ref, v_ref, g_ref, gain_ref, bias_ref, rk_ref, o_ref):
    y = y_ref[...]
    d = y - _head_sum(y) * (1.0 / HD_B)
    var = _head_sum(d * d) * (1.0 / HD_B)
    yn = d * lax.rsqrt(var + GN_EPS_B) * gain_ref[...] + bias_ref[...]
    v = v_ref[...]
    bonus = _head_sum(r_ref[...] * k_ref[...] * rk_ref[...]) * v
    o_ref[...] = ((yn + bonus) * g_ref[...]).astype(o_ref.dtype)


def _post_b(y, r, k, v, g, gain, bias, r_k, tm):
    m = y.shape[0]
    tile = pl.BlockSpec((tm, D_B), lambda i: (i, 0))
    row = pl.BlockSpec((1, D_B), lambda i: (0, 0))
    return pl.pallas_call(
        _post_b_kernel,
        grid=(m // tm,),
        in_specs=[tile] * 5 + [row] * 3,
        out_specs=tile,
        out_shape=jax.ShapeDtypeStruct((m, D_B), BF16),
        compiler_params=_params(("parallel",)),
        name="rwkv_post",
    )(y, r, k, v, g, gain, bias, r_k)


def _lambda(lam_ref, lam_init):
    l4 = lam_ref[...]
    s1 = jnp.sum(l4[0:1, :] * l4[1:2, :], axis=1, keepdims=True)
    s2 = jnp.sum(l4[2:3, :] * l4[3:4, :], axis=1, keepdims=True)
    return jnp.exp(s1) - jnp.exp(s2) + lam_init


def _diff_combine(o0, o1, lam, subw, lam_init):
    o = o0 - lam * o1
    o = o * lax.rsqrt(jnp.mean(o * o, axis=-1, keepdims=True) + SUBLN_EPS) * subw
    return o * (1.0 - lam_init)


def _flash_kernel(q_ref, k_ref, v_ref, lam_ref, subw_ref, o_ref, m_scr, l_scr, acc_scr, *, tq, tk, lam_init):
    qi = pl.program_id(1)
    m_scr[...] = jnp.full_like(m_scr, NEG_BIG)
    l_scr[...] = jnp.zeros_like(l_scr)
    acc_scr[...] = jnp.zeros_like(acc_scr)

    def block(off, diag):
        kb = k_ref[pl.ds(off, tk), :]
        vb = v_ref[pl.ds(off, tk), :]
        if diag is not None:
            row = lax.broadcasted_iota(jnp.int32, (tq, tk), 0)
            col = lax.broadcasted_iota(jnp.int32, (tq, tk), 1) + diag * tk
            keep = col <= row
        for i in range(2):
            sl = slice(i * HD_C, (i + 1) * HD_C)
            s = lax.dot_general(q_ref[:, sl], kb[:, sl], (((1,), (1,)), ((), ())),
                                preferred_element_type=F32)
            if diag is not None:
                s = jnp.where(keep, s, NEG_BIG)
            m_old = m_scr[i]
            m_new = jnp.maximum(m_old, jnp.max(s, axis=1, keepdims=True))
            corr = jnp.exp2(m_old - m_new)
            p = jnp.exp2(s - jnp.concatenate([m_new] * (tk // LANES), axis=1))
            l_scr[i] = l_scr[i] * corr + jnp.sum(p, axis=1, keepdims=True)
            acc_scr[i] = (acc_scr[i] * jnp.concatenate([corr] * (DV_C // LANES), axis=1)
                          + jnp.dot(p.astype(BF16), vb, preferred_element_type=F32))
            m_scr[i] = m_new

    def full_block(j, carry):
        block(pl.multiple_of(j * tk, tk), None)
        return carry

    lax.fori_loop(0, (qi * tq) // tk, full_block, 0)
    for d in range(tq // tk):
        block(pl.multiple_of(qi * tq + d * tk, tk), d)

    lam = _lambda(lam_ref, lam_init)
    o0 = acc_scr[0] / jnp.concatenate([l_scr[0]] * (DV_C // LANES), axis=1)
    o1 = acc_scr[1] / jnp.concatenate([l_scr[1]] * (DV_C // LANES), axis=1)
    o_ref[...] = _diff_combine(o0, o1, lam, subw_ref[...], lam_init).astype(o_ref.dtype)


def _flash_prompt(q, k, v, lam4, subw, lam_init, tq, tk):
    s = q.shape[0]
    blk = 2 * HD_C
    return pl.pallas_call(
        functools.partial(_flash_kernel, tq=tq, tk=tk, lam_init=lam_init),
        grid=(H_C, s // tq),
        in_specs=[pl.BlockSpec((tq, blk), lambda h, i: (i, h)),
                  pl.BlockSpec((s, blk), lambda h, i: (0, h)),
                  pl.BlockSpec((s, DV_C), lambda h, i: (0, h)),
                  pl.BlockSpec((4, HD_C), lambda h, i: (0, 0)),
                  pl.BlockSpec((1, DV_C), lambda h, i: (0, 0))],
        out_specs=pl.BlockSpec((tq, DV_C), lambda h, i: (i, h)),
        out_shape=jax.ShapeDtypeStruct((s, D_V_C), BF16),
        scratch_shapes=[pltpu.VMEM((2, tq, LANES), F32), pltpu.VMEM((2, tq, LANES), F32),
                        pltpu.VMEM((2, tq, DV_C), F32)],
        compiler_params=_params(("parallel", "arbitrary")),
        name="diff_attn_prompt",
    )(q, k, v, lam4, subw)


DECODE_PAGES = 4
WKV_CHUNK = 128


def _decode_kernel(pt_ref, q_ref, kn_ref, vn_ref, *rest, steps, lam_init):
    del pt_ref
    kc_refs, vc_refs = rest[:DECODE_PAGES], rest[DECODE_PAGES:2 * DECODE_PAGES]
    lam_ref, subw_ref, o_ref, qrow_scr, m_scr, l_scr, acc_scr = rest[2 * DECODE_PAGES:]
    pj = pl.program_id(1)
    nrow = 2 * H_C * steps
    grp = 2 * steps

    @pl.when(pj == 0)
    def _():
        q = jnp.tile(q_ref[...], (2 * H_C, 1))
        r = lax.broadcasted_iota(jnp.int32, q.shape, 0) // steps
        c = lax.broadcasted_iota(jnp.int32, q.shape, 1) // HD_C
        qrow_scr[...] = jnp.where(r == c, q, 0.0).astype(BF16)
        m_scr[...] = jnp.full_like(m_scr, NEG_BIG)
        l_scr[...] = jnp.zeros_like(l_scr)
        acc_scr[...] = jnp.zeros_like(acc_scr)

    def update(k_bf, v_of_head, keep):
        s = lax.dot_general(qrow_scr[...], k_bf, (((1,), (1,)), ((), ())), preferred_element_type=F32)
        if keep is not None:
            s = jnp.where(keep, s, NEG_BIG)
        m_old = m_scr[...]
        m_new = jnp.maximum(m_old, jnp.max(s, axis=1, keepdims=True))
        corr = jnp.exp(m_old - m_new)
        p = jnp.exp(s - m_new)
        l_scr[...] = l_scr[...] * corr + jnp.sum(p, axis=1, keepdims=True)
        m_scr[...] = m_new
        pb = p.astype(BF16)
        for h in range(H_C):
            rows = slice(h * grp, (h + 1) * grp)
            acc_scr[rows, :] = acc_scr[rows, :] * corr[rows, :] + jnp.dot(
                pb[rows, :], v_of_head(h), preferred_element_type=F32)

    def cached_k(ref):
        return jnp.concatenate(
            [ref[pl.ds(hi, PAGE_SIZE, stride=2 * H_C), :].astype(BF16) for hi in range(2 * H_C)], axis=1)

    def cached_v(h):
        nlt = DV_C // LANES
        return jnp.concatenate(
            [jnp.concatenate([ref[pl.ds(lt * H_C + h, PAGE_SIZE, stride=nlt * H_C), :].astype(BF16)
                              for lt in range(nlt)], axis=1) for ref in vc_refs], axis=0)

    update(jnp.concatenate([cached_k(ref) for ref in kc_refs], axis=0), cached_v, None)

    @pl.when(pj == pl.num_programs(1) - 1)
    def _():
        pad = jnp.zeros((PAGE_SIZE - steps, kn_ref.shape[1]), F32)
        kn = jnp.concatenate([kn_ref[...], pad], axis=0).astype(BF16)
        vn = jnp.concatenate([vn_ref[...], pad], axis=0).astype(BF16)
        step = lax.broadcasted_iota(jnp.int32, (nrow, PAGE_SIZE), 0) % steps
        key = lax.broadcasted_iota(jnp.int32, (nrow, PAGE_SIZE), 1)
        update(kn, lambda h: vn[:, h * DV_C:(h + 1) * DV_C], key <= step)
        lam = _lambda(lam_ref, lam_init)
        o = acc_scr[...] / l_scr[...]
        for h in range(H_C):
            o0 = o[h * grp:h * grp + steps, :]
            o1 = o[h * grp + steps:(h + 1) * grp, :]
            o_ref[:, h * DV_C:(h + 1) * DV_C] = _diff_combine(o0, o1, lam, subw_ref[...], lam_init)


def _decode_attn(q, kn, vn, cache_k, cache_v, page_table, lam4, subw, lam_init, steps):
    bsz, n_pages = page_table.shape
    nrow = 2 * H_C * steps
    tile = lambda n: pl.BlockSpec((steps, n), lambda b, p, pt: (b, 0))

    def page(shape, u):
        return pl.BlockSpec((None,) + shape,
                            lambda b, p, pt: (pt[b * n_pages + p * DECODE_PAGES + u],) + (0,) * len(shape))

    grid_spec = pltpu.PrefetchScalarGridSpec(
        num_scalar_prefetch=1,
        grid=(bsz, n_pages // DECODE_PAGES),
        in_specs=([tile(D_QK_C), tile(D_QK_C), tile(D_V_C)]
                  + [page((PAGE_SIZE * 2 * H_C, HD_C), u) for u in range(DECODE_PAGES)]
                  + [page((PAGE_SIZE * H_C * DV_C // LANES, LANES), u) for u in range(DECODE_PAGES)]
                  + [pl.BlockSpec((4, HD_C), lambda b, p, pt: (0, 0)),
                     pl.BlockSpec((1, DV_C), lambda b, p, pt: (0, 0))]),
        out_specs=tile(D_V_C),
        scratch_shapes=[pltpu.VMEM((nrow, D_QK_C), BF16), pltpu.VMEM((nrow, 1), F32),
                        pltpu.VMEM((nrow, 1), F32), pltpu.VMEM((nrow, DV_C), F32)],
    )
    return pl.pallas_call(
        functools.partial(_decode_kernel, steps=steps, lam_init=lam_init),
        grid_spec=grid_spec,
        out_shape=jax.ShapeDtypeStruct((bsz * steps, D_V_C), F32),
        compiler_params=_params(("parallel", "arbitrary")),
        name="diff_attn_decode",
    )(page_table.reshape(-1), q, kn, vn, *([cache_k] * DECODE_PAGES), *([cache_v] * DECODE_PAGES), lam4, subw)


def _ffn_up_kernel(h_ref, wg_ref, wv_ref, cwg_ref, cwv_ref, cbg_ref, cbv_ref, bg_ref, bv_ref,
                   o_ref, ng_ref, nv_ref, cg_scr, cv_scr, *, nb, tt, carried):
    c = wg_ref.shape[-1]
    h = h_ref[...]

    def branch(w_ref, cw_ref, cb_ref, buf_ref, new_ref, carry_ref):
        u = jnp.dot(h, w_ref[...], preferred_element_type=F32).reshape(nb, tt, c)
        if carried:
            @pl.when(pl.program_id(1) == 0)
            def _():
                carry_ref[...] = buf_ref[...]
            prev = carry_ref[...]
        else:
            prev = buf_ref[...]
        y = _conv3(u, prev, cw_ref[...]) + cb_ref[...]
        tail = u[:, tt - 2:tt, :]
        new_ref[...] = tail
        if carried:
            carry_ref[...] = tail
        return y.reshape(nb * tt, c)

    gate = branch(wg_ref, cwg_ref, cbg_ref, bg_ref, ng_ref, cg_scr)
    val = branch(wv_ref, cwv_ref, cbv_ref, bv_ref, nv_ref, cv_scr)
    o_ref[...] = (jax.nn.silu(gate) * val).astype(o_ref.dtype)


def _ffn_up(h, w_up, conv_w, conv_b, buf, seq_len, tm, tn):
    m, d = h.shape
    nj = D_FF // tn
    nb, tt = _seq_tiles(tm, seq_len)
    carried = nb == 1
    bsz = buf.shape[0]

    def cols(off):
        return lambda j, i: (0, j + off)

    def buf_spec(off):
        if carried:
            return pl.BlockSpec((1, 2, tn), lambda j, i: (0, 0, j + off))
        return pl.BlockSpec((nb, 2, tn), lambda j, i: (i, 0, j + off))

    return pl.pallas_call(
        functools.partial(_ffn_up_kernel, nb=nb, tt=tt, carried=carried),
        grid=(nj, m // tm),
        in_specs=[pl.BlockSpec((tm, d), lambda j, i: (i, 0)),
                  pl.BlockSpec((d, tn), cols(0)), pl.BlockSpec((d, tn), cols(nj)),
                  pl.BlockSpec((CONV_W, tn), cols(0)), pl.BlockSpec((CONV_W, tn), cols(nj)),
                  pl.BlockSpec((1, tn), cols(0)), pl.BlockSpec((1, tn), cols(nj)),
                  buf_spec(0), buf_spec(nj)],
        out_specs=[pl.BlockSpec((tm, tn), lambda j, i: (i, j)), buf_spec(0), buf_spec(0)],
        out_shape=[jax.ShapeDtypeStruct((m, D_FF), BF16),
                   jax.ShapeDtypeStruct((bsz, 2, D_FF), F32), jax.ShapeDtypeStruct((bsz, 2, D_FF), F32)],
        scratch_shapes=[pltpu.VMEM((1, 2, tn), F32), pltpu.VMEM((1, 2, tn), F32)],
        compiler_params=_params(("parallel", "arbitrary")),
        name="ffn_up",
    )(h, w_up, w_up, conv_w, conv_w, conv_b, conv_b, buf, buf)


def _pad_cols(x, n):
    return jnp.pad(x, [(0, 0)] * (x.ndim - 1) + [(0, n - x.shape[-1])])


def _trunk(x, conv_a, shift_b, wkv_b, ffn_buf, cache_k, cache_v, page_table, P, seq_len):
    m = x.shape[0]
    bsz = m // seq_len
    tm = 1024
    tm_small = 256

    h = _rms(x, P["norm_mix"][0], BF16)
    pa = _mm([h], [P["w_in_a"]], [(0, 0)], P_A, tm=tm, tn=1024, name="proj_in_a")
    pb = _mm([h], [P["w_in_b"]], [(0, 0)], P_B_PAD, tm=tm, tn=P_B_PAD // 3, name="proj_in_b")
    y_a, new_conv = _mixer_a(pa, P["conv_w_a"], conv_a, seq_len, 512)
    (r, lw, k2, v, kk, b, g, new_shift) = _prep_b(
        pb, shift_b, P["mu_b"], P["w0_b"], P["a0_b"], P["k_k_b"], P["k_a_b"],
        P["w2_b"], P["a2_b"], P["g2_b"], seq_len, tm_small)
    if seq_len % WKV_CHUNK == 0:
        y, new_wkv = _wkv_chunked(r, lw, k2, v, kk, b, wkv_b, seq_len, WKV_CHUNK)
    else:
        y_t, new_wkv = _wkv(r, lw, k2, v, kk, b, wkv_b, seq_len, seq_len)
        y = jnp.swapaxes(y_t, 1, 2).reshape(m, D_B)
    y_b = _post_b(y, r, k2, v, g, P["ln_gain_b"], P["ln_bias_b"], P["r_k_b"], tm_small)
    x = _mm([y_a, y_b], [P["w_out_ab"], P["w_out_ab"]], [(0, 0), (1, 0)], D_MODEL,
            tm=tm, tn=1024, res=x, name="proj_out_ab")

    h = _rms(x, P["norm_ffn"][0], BF16)
    gated, nfg0, nfv0 = _ffn_up(h, P["w_up_f"][0], P["conv_w_f"][0], P["conv_b_f"][0], ffn_buf[0],
                                seq_len, tm, 512)
    x = _mm([gated], [P["w_down_f"][0]], [(0, 0)], D_MODEL, tm=tm, tn=512, res=x, name="ffn_down")

    lam_init = 0.8 - 0.6 * math.exp(-0.3 * 1)
    h = _rms(x, P["norm_mix"][1], BF16)
    nq = D_QK_C // 1024
    if cache_k is None:
        qh = _mm([h], [P["w_qkv_c"]], [(0, 0)], D_QK_C, tm=tm, tn=1024, out_dtypes=(BF16,),
                 scale=ATTN_SCALE * math.log2(math.e), name="proj_q")
        k_rows, k_bf = _mm([h], [P["w_qkv_c"]], [(0, nq)], D_QK_C, tm=tm, tn=1024,
                           out_dtypes=(F32, BF16), name="proj_k")
        v_rows, v_bf = _mm([h], [P["w_qkv_c"]], [(0, 2 * nq)], D_V_C, tm=tm, tn=1024,
                           out_dtypes=(F32, BF16), name="proj_v")
        attn = _flash_prompt(qh, k_bf, v_bf, P["lam4"], P["subln_w_c"], lam_init, 1024, 512)
    else:
        qh = _mm([h], [P["w_qkv_c"]], [(0, 0)], D_QK_C, tm=tm, tn=1024, scale=ATTN_SCALE, name="proj_q")
        k_rows = _mm([h], [P["w_qkv_c"]], [(0, nq)], D_QK_C, tm=tm, tn=1024, name="proj_k")
        v_rows = _mm([h], [P["w_qkv_c"]], [(0, 2 * nq)], D_V_C, tm=tm, tn=1024, name="proj_v")
        attn = _decode_attn(qh, k_rows, v_rows, cache_k, cache_v, page_table, P["lam4"], P["subln_w_c"],
                            lam_init, seq_len)
    x = _mm([attn], [P["w_out_c"]], [(0, 0)], D_MODEL, tm=tm, tn=1024, res=x, name="proj_out_c")

    h = _rms(x, P["norm_ffn"][1], BF16)
    gated, nfg1, nfv1 = _ffn_up(h, P["w_up_f"][1], P["conv_w_f"][1], P["conv_b_f"][1], ffn_buf[1],
                                seq_len, tm, 512)
    x = _mm([gated], [P["w_down_f"][1]], [(0, 0)], D_MODEL, tm=tm, tn=512, res=x, name="ffn_down")

    y_out = _rms(x, P["norm_final"], F32)
    new_ffn = jnp.stack([jnp.concatenate([nfg0, nfv0], axis=-1), jnp.concatenate([nfg1, nfv1], axis=-1)])
    return (y_out.reshape(bsz, seq_len, D_MODEL),
            new_conv[None],
            new_shift[None, :, 0, :P_B],
            new_wkv[None],
            k_rows.reshape(1, bsz, seq_len, 2 * H_C, HD_C),
            v_rows.reshape(1, bsz, seq_len, H_C, DV_C),
            new_ffn)


def kernel(x_prompt, x_sample, state_conv_a, state_shift_b, state_wkv_b, cache_k, cache_v, page_table,
           state_ffn_conv, norm_mix, norm_ffn, norm_final, w_in_ab, conv_w_a, mu_b, w0_b, w2_b, a0_b, a2_b,
           g2_b, k_k_b, k_a_b, r_k_b, ln_gain_b, ln_bias_b, w_out_ab, w_qkv_c, lambda_q1_c, lambda_k1_c,
           lambda_q2_c, lambda_k2_c, subln_w_c, w_out_c, w_up_f, conv_w_f, conv_b_f, w_down_f):
    bp, seq, d = x_prompt.shape
    bs, dec_seq, _ = x_sample.shape
    row = lambda a: a.reshape(1, -1)
    lora_rows = jnp.zeros((LORA_PAD, D_B), F32)
    P = dict(
        norm_mix=norm_mix, norm_ffn=norm_ffn, norm_final=norm_final,
        w_in_a=w_in_ab[0, :, :P_A].astype(BF16),
        w_in_b=_pad_cols(w_in_ab[0, :, P_A:], P_B_PAD).astype(BF16),
        conv_w_a=conv_w_a[0],
        mu_b=_pad_cols(row(mu_b[0]), P_B_PAD),
        w0_b=row(w0_b[0]), a0_b=row(a0_b[0]), k_k_b=row(k_k_b[0]), k_a_b=row(k_a_b[0]),
        w2_b=lora_rows.at[:R_W].set(w2_b[0]).astype(BF16),
        a2_b=lora_rows.at[R_W:R_W + R_A].set(a2_b[0]).astype(BF16),
        g2_b=lora_rows.at[R_W + R_A:R_W + R_A + R_G].set(g2_b[0]).astype(BF16),
        r_k_b=row(r_k_b[0]), ln_gain_b=row(ln_gain_b[0]), ln_bias_b=row(ln_bias_b[0]),
        w_out_ab=w_out_ab[0].astype(BF16),
        w_qkv_c=w_qkv_c[0].astype(BF16),
        lam4=jnp.stack([lambda_q1_c[0], lambda_k1_c[0], lambda_q2_c[0], lambda_k2_c[0]]),
        subln_w_c=row(subln_w_c[0]),
        w_out_c=w_out_c[0].astype(BF16),
        w_up_f=w_up_f.astype(BF16), conv_w_f=conv_w_f, conv_b_f=conv_b_f[:, None, :],
        w_down_f=w_down_f.astype(BF16),
    )
    n_phys = cache_k.shape[1]
    outs_p = _trunk(
        x_prompt.reshape(bp * seq, d),
        jnp.zeros((bp, CONV_W - 1, D_A), F32),
        jnp.zeros((bp, 1, P_B_PAD), F32),
        jnp.zeros((bp, H_B, HD_B, HD_B), F32),
        jnp.zeros((DEPTH, bp, CONV_W - 1, 2 * D_FF), F32),
        None, None, None, P, seq)
    outs_s = _trunk(
        x_sample.reshape(bs * dec_seq, d),
        state_conv_a[0],
        _pad_cols(state_shift_b[0], P_B_PAD)[:, None, :],
        state_wkv_b[0],
        state_ffn_conv,
        cache_k.reshape(n_phys, PAGE_SIZE * 2 * H_C, HD_C),
        cache_v.reshape(n_phys, PAGE_SIZE, H_C, DV_C // LANES, LANES).swapaxes(2, 3).reshape(
            n_phys, PAGE_SIZE * H_C * DV_C // LANES, LANES),
        page_table, P, dec_seq)
    return (outs_p[0], outs_s[0]) + outs_p[1:] + outs_s[1:]
```

```python
import functools
import math

import jax
import jax.numpy as jnp
from jax import lax
from jax.experimental import pallas as pl
from jax.experimental.pallas import tpu as pltpu

F32 = jnp.float32
BF16 = jnp.bfloat16

D_MODEL = 2048
DEPTH = 2
PAGE_SIZE = 128
CONV_W = 3
NORM_EPS = 1e-6
D_A = D_MODEL // 2
P_A = 3 * D_A
D_B = D_MODEL // 2
HD_B = 64
H_B = D_B // HD_B
R_W = 64
R_A = 64
R_G = 160
P_B = 3 * D_B + R_W + R_A + R_G
LORA_PAD = 384
P_B_PAD = 3 * D_B + LORA_PAD
GN_EPS_B = 64e-5
HD_C = 128
H_C = D_MODEL // (2 * HD_C)
DV_C = 2 * HD_C
D_QK_C = 2 * H_C * HD_C
D_V_C = H_C * DV_C
ATTN_SCALE = HD_C ** -0.5
SUBLN_EPS = 1e-5
D_FF = 5632

LANES = 128
SUBLANES = 8
VMEM_LIMIT = 52 * 1024 * 1024
NEG_BIG = -1e30


def _params(sem):
    return pltpu.CompilerParams(dimension_semantics=sem, vmem_limit_bytes=VMEM_LIMIT)


def _rms_kernel(x_ref, g_ref, o_ref):
    x = x_ref[...]
    y = x * lax.rsqrt(jnp.mean(x * x, axis=-1, keepdims=True) + NORM_EPS)
    o_ref[...] = (y * g_ref[...]).astype(o_ref.dtype)


def _rms(x, g, out_dtype, tm=512):
    m, d = x.shape
    return pl.pallas_call(
        _rms_kernel,
        grid=(m // tm,),
        in_specs=[pl.BlockSpec((tm, d), lambda i: (i, 0)), pl.BlockSpec((1, d), lambda i: (0, 0))],
        out_specs=pl.BlockSpec((tm, d), lambda i: (i, 0)),
        out_shape=jax.ShapeDtypeStruct((m, d), out_dtype),
        compiler_params=_params(("parallel",)),
        name="rmsnorm",
    )(x, g.reshape(1, d))


def _mm_kernel(*refs, nx, has_res, has_norm, scale):
    xs, ws = refs[:nx], refs[nx:2 * nx]
    pos = 2 * nx
    res = refs[pos] if has_res else None
    pos += int(has_res)
    gain = refs[pos] if has_norm else None
    pos += int(has_norm)
    outs = refs[pos:len(refs) - int(has_norm)]
    if has_norm:
        h_scr = refs[-1]

        @pl.when(pl.program_id(1) == 0)
        def _():
            x = xs[0][...]
            y = x * lax.rsqrt(jnp.mean(x * x, axis=-1, keepdims=True) + NORM_EPS)
            h_scr[...] = (y * gain[...]).astype(h_scr.dtype)

        lhs = [h_scr[...]]
    else:
        lhs = [x[...].astype(BF16) for x in xs]
    acc = None
    for x, w in zip(lhs, ws):
        d = jnp.dot(x, w[...], preferred_element_type=F32)
        acc = d if acc is None else acc + d
    if scale is not None:
        acc = acc * scale
    if res is not None:
        acc = acc + res[...]
    for o in outs:
        o[...] = acc.astype(o.dtype)


def _mm(xs, ws, wblocks, n, *, tm, tn, out_dtypes=(F32,), res=None, scale=None, norm_gain=None, name="matmul"):
    m = xs[0].shape[0]
    in_specs, args, scratch = [], [], []
    for x in xs:
        in_specs.append(pl.BlockSpec((tm, x.shape[1]), lambda i, j: (i, 0)))
        args.append(x)
    for x, w, (rb, cb) in zip(xs, ws, wblocks):
        in_specs.append(pl.BlockSpec((x.shape[1], tn), lambda i, j, rb=rb, cb=cb: (rb, cb + j)))
        args.append(w)
    if res is not None:
        in_specs.append(pl.BlockSpec((tm, tn), lambda i, j: (i, j)))
        args.append(res)
    if norm_gain is not None:
        assert len(xs) == 1
        d = xs[0].shape[1]
        in_specs.append(pl.BlockSpec((1, d), lambda i, j: (0, 0)))
        args.append(norm_gain.reshape(1, d))
        scratch.append(pltpu.VMEM((tm, d), BF16))
    outs = pl.pallas_call(
        functools.partial(_mm_kernel, nx=len(xs), has_res=res is not None, has_norm=norm_gain is not None,
                          scale=scale),
        grid=(m // tm, n // tn),
        in_specs=in_specs,
        out_specs=[pl.BlockSpec((tm, tn), lambda i, j: (i, j)) for _ in out_dtypes],
        out_shape=[jax.ShapeDtypeStruct((m, n), dt) for dt in out_dtypes],
        scratch_shapes=scratch,
        compiler_params=_params(("parallel", "arbitrary")),
        name=name,
    )(*args)
    return outs if len(out_dtypes) > 1 else outs[0]


def _seq_tiles(tm, seq_len):
    return (tm // seq_len, seq_len) if seq_len < tm else (1, tm)


def _shift_rows(u3, prev, k):
    out = pltpu.roll(u3, k, 1)
    head = out[:, :SUBLANES, :]
    t = lax.broadcasted_iota(jnp.int32, head.shape, 1)
    for s in range(k):
        head = jnp.where(t == s, prev[:, 2 - k + s:3 - k + s, :], head)
    if u3.shape[1] == SUBLANES:
        return head
    return jnp.concatenate([head, out[:, SUBLANES:, :]], axis=1)


def _conv3(u3, prev, w):
    return w[0:1, :] * _shift_rows(u3, prev, 2) + w[1:2, :] * _shift_rows(u3, prev, 1) + w[2:3, :] * u3


def _mixer_a_kernel(gb_ref, gc_ref, xt_ref, w_ref, buf_ref, ya_ref, nc_ref, carry_ref, *, nb, tt, carried):
    c = gb_ref.shape[-1]
    cx = (gc_ref[...] * xt_ref[...]).reshape(nb, tt, c)
    if carried:
        @pl.when(pl.program_id(0) == 0)
        def _():
            carry_ref[...] = buf_ref[...]
        prev = carry_ref[...]
    else:
        prev = buf_ref[...]
    conv = _conv3(cx, prev, w_ref[...])
    ya_ref[...] = (gb_ref[...] * conv.reshape(nb * tt, c)).astype(ya_ref.dtype)
    tail = cx[:, tt - 2:tt, :]
    nc_ref[...] = tail
    if carried:
        carry_ref[...] = tail


def _mixer_a(pa, conv_w, buf, seq_len, tm):
    m = pa.shape[0]
    nb, tt = _seq_tiles(tm, seq_len)
    carried = nb == 1
    bsz = buf.shape[0]
    buf_spec = (pl.BlockSpec((1, 2, D_A), lambda i: (0, 0, 0)) if carried
                else pl.BlockSpec((nb, 2, D_A), lambda i: (i, 0, 0)))
    ya, nc = pl.pallas_call(
        functools.partial(_mixer_a_kernel, nb=nb, tt=tt, carried=carried),
        grid=(m // tm,),
        in_specs=[pl.BlockSpec((tm, D_A), lambda i: (i, 0)),
                  pl.BlockSpec((tm, D_A), lambda i: (i, 1)),
                  pl.BlockSpec((tm, D_A), lambda i: (i, 2)),
                  pl.BlockSpec((CONV_W, D_A), lambda i: (0, 0)),
                  buf_spec],
        out_specs=[pl.BlockSpec((tm, D_A), lambda i: (i, 0)), buf_spec],
        out_shape=[jax.ShapeDtypeStruct((m, D_A), BF16), jax.ShapeDtypeStruct((bsz, 2, D_A), F32)],
        scratch_shapes=[pltpu.VMEM((1, 2, D_A), F32)],
        compiler_params=_params(("arbitrary",)),
        name="mixer_a",
    )(pa, pa, pa, conv_w, buf)
    return ya, nc


def _split_bf16(x):
    hi = x.astype(BF16)
    lo = (x - hi.astype(F32)).astype(BF16)
    return hi, lo


def _head_sum(x):
    blk = 2 * LANES
    r = lax.broadcasted_iota(jnp.int32, (blk, blk), 0) // HD_B
    c = lax.broadcasted_iota(jnp.int32, (blk, blk), 1) // HD_B
    ones = jnp.where(r == c, 1.0, 0.0).astype(BF16)
    hi, lo = _split_bf16(x)
    parts = []
    for s in range(x.shape[1] // blk):
        sl = slice(s * blk, (s + 1) * blk)
        parts.append(jnp.dot(hi[:, sl], ones, preferred_element_type=F32)
                     + jnp.dot(lo[:, sl], ones, preferred_element_type=F32))
    return jnp.concatenate(parts, axis=1)


def _prep_b_kernel(pb_ref, prev_ref, mu_ref, w0_ref, a0_ref, kk_ref, ka_ref, w2_ref, a2_ref, g2_ref,
                   r_out, lw_out, k_out, v_out, kk_out, b_out, g_out, ns_out,
                   carry_ref, *, nb, tt, carried):
    c = pb_ref.shape[-1]
    pb3 = pb_ref[...].reshape(nb, tt, c)
    if carried:
        @pl.when(pl.program_id(0) == 0)
        def _():
            carry_ref[...] = prev_ref[...]
        prev = carry_ref[...]
    else:
        prev = prev_ref[...]
    t = lax.broadcasted_iota(jnp.int32, pb3.shape, 1)
    shifted = jnp.where(t == 0, prev, pltpu.roll(pb3, 1, 1))
    last = pb3[:, tt - 1:tt, :]
    ns_out[...] = last
    if carried:
        carry_ref[...] = last
    x = (pb3 + mu_ref[...] * (shifted - pb3)).reshape(nb * tt, c)
    r = x[:, :D_B]
    k = x[:, D_B:2 * D_B]
    v = x[:, 2 * D_B:3 * D_B]
    lo = x[:, 3 * D_B:]
    wl = w0_ref[...] + jnp.dot(jnp.tanh(lo).astype(BF16), w2_ref[...], preferred_element_type=F32)
    softplus_neg = jnp.maximum(-wl, 0.0) + jnp.log(1.0 + jnp.exp(-jnp.abs(wl)))
    w_log = -softplus_neg - 0.5
    a = jax.nn.sigmoid(a0_ref[...] + jnp.dot(lo.astype(BF16), a2_ref[...], preferred_element_type=F32))
    g = jnp.dot(jax.nn.sigmoid(lo).astype(BF16), g2_ref[...], preferred_element_type=F32)
    kk = k * kk_ref[...]
    kk = kk / jnp.maximum(jnp.sqrt(_head_sum(kk * kk)), 1e-12)
    k2 = k * (1.0 + (a - 1.0) * ka_ref[...])
    b = kk * a
    r_out[...] = r
    lw_out[...] = -jnp.exp(w_log)
    k_out[...] = k2
    v_out[...] = v
    kk_out[...] = kk
    b_out[...] = b
    g_out[...] = g


def _prep_b(pb, shift_prev, mu, w0, a0, k_k, k_a, w2p, a2p, g2p, seq_len, tm):
    m = pb.shape[0]
    nb, tt = _seq_tiles(tm, seq_len)
    carried = nb == 1
    bsz = shift_prev.shape[0]
    prev_spec = (pl.BlockSpec((1, 1, P_B_PAD), lambda i: (0, 0, 0)) if carried
                 else pl.BlockSpec((nb, 1, P_B_PAD), lambda i: (i, 0, 0)))
    row = lambda n: pl.BlockSpec((1, n), lambda i: (0, 0))
    lora = pl.BlockSpec((LORA_PAD, D_B), lambda i: (0, 0))
    tile = pl.BlockSpec((tm, D_B), lambda i: (i, 0))
    outs = pl.pallas_call(
        functools.partial(_prep_b_kernel, nb=nb, tt=tt, carried=carried),
        grid=(m // tm,),
        in_specs=[pl.BlockSpec((tm, P_B_PAD), lambda i: (i, 0)), prev_spec, row(P_B_PAD),
                  row(D_B), row(D_B), row(D_B), row(D_B), lora, lora, lora],
        out_specs=[tile] * 7 + [prev_spec],
        out_shape=[jax.ShapeDtypeStruct((m, D_B), F32)] * 7 + [jax.ShapeDtypeStruct((bsz, 1, P_B_PAD), F32)],
        scratch_shapes=[pltpu.VMEM((1, 1, P_B_PAD), F32)],
        compiler_params=_params(("arbitrary",)),
        name="rwkv_prep",
    )(pb, shift_prev, mu, w0, a0, k_k, k_a, w2p, a2p, g2p)
    return outs


def _split3_bf16(x):
    hi = x.astype(BF16)
    rest = x - hi.astype(F32)
    mid = rest.astype(BF16)
    return hi, mid, (rest - mid.astype(F32)).astype(BF16)


def _dot_bf16(a, b, dims=(((1,), (0,)), ((), ()))):
    return lax.dot_general(a.astype(BF16), b.astype(BF16), dims, preferred_element_type=F32)


_NT = (((1,), (1,)), ((), ()))
_TN = (((0,), (0,)), ((), ()))
WKV_CHUNK = 128
WKV_SHORT_NSEQ = 8


def _wkv_chunk_kernel(r_ref, lw_ref, k_ref, v_ref, kk_ref, b_ref, s0_ref, y_ref, sf_ref, s_scr, *, c, nseq):
    ci = pl.program_id(1)
    npair = H_B // 2
    rows = nseq * c

    @pl.when(ci == 0)
    def _():
        s_scr[...] = jnp.zeros_like(s_scr)
        for q in range(nseq):
            for j in range(npair):
                s_scr[q * npair + j, 0:HD_B, 0:HD_B] = s0_ref[q, 2 * j]
                s_scr[q * npair + j, HD_B:, HD_B:] = s0_ref[q, 2 * j + 1]

    ti = lax.broadcasted_iota(jnp.int32, (rows, rows), 0)
    tj = lax.broadcasted_iota(jnp.int32, (rows, rows), 1)
    tri = jnp.where(jnp.logical_and(ti // c == tj // c, tj <= ti), 1.0, 0.0).astype(BF16)
    lw = lw_ref[...]
    cum = None
    for limb in _split3_bf16(lw):
        d = jnp.dot(tri, limb, preferred_element_type=F32)
        cum = d if cum is None else cum + d
    cum3 = cum.reshape(nseq, c, D_B)
    last3 = cum3[:, c - 1:c, :]
    mid3 = cum3[:, c // 2 - 1:c // 2, :]
    cc = (cum3 - mid3).reshape(rows, D_B)
    g_mid3 = jnp.exp(mid3)
    g_last3 = jnp.exp(last3)
    g_inv = jnp.exp(-cc)
    kt = kk_ref[...] * jnp.exp(cc - lw)
    rt = r_ref[...] * jnp.exp(cc)
    bt = b_ref[...] * g_inv
    kq = k_ref[...] * g_inv
    to_end = jnp.exp(last3 - cum3).reshape(rows, D_B)
    kh = k_ref[...] * to_end
    bh = b_ref[...] * to_end
    v = v_ref[...]

    ci_ = lax.broadcasted_iota(jnp.int32, (c, c), 0)
    cj_ = lax.broadcasted_iota(jnp.int32, (c, c), 1)
    first = lax.broadcasted_iota(jnp.int32, (c, LANES), 1) < HD_B
    col2 = lax.broadcasted_iota(jnp.int32, (c, 2 * c), 1)
    row2 = lax.broadcasted_iota(jnp.int32, (c, 2 * c), 0)
    strict = (col2 % c) < row2
    incl = (col2 % c) <= row2
    second_grp = col2 >= c
    strict_sq = cj_ < ci_
    blockdiag = (lax.broadcasted_iota(jnp.int32, (LANES, LANES), 0) // HD_B
                 == lax.broadcasted_iota(jnp.int32, (LANES, LANES), 1) // HD_B)
    lane_l = lax.broadcasted_iota(jnp.int32, (2 * c, LANES), 1)
    nsq = (c - 1).bit_length() - 1

    units = [(q, j) for q in range(nseq) for j in range(npair)]
    rsl = [slice(q * c, (q + 1) * c) for q, j in units]
    sls = [slice(j * LANES, (j + 1) * LANES) for q, j in units]
    s2s = [s_scr[q * npair + j] for q, j in units]
    lhs2s = [jnp.concatenate([kt[rs, sl], rt[rs, sl]], axis=0) for rs, sl in zip(rsl, sls)]
    rhs2s = [jnp.concatenate([bt[rs, sl], kq[rs, sl]], axis=0) for rs, sl in zip(rsl, sls)]
    vps = [v[rs, sl] for rs, sl in zip(rsl, sls)]
    wy0s = [_dot_bf16(lhs2s[n], s2s[n] * g_mid3[q][:, sls[n]], _NT)
            for n, (q, j) in enumerate(units)]
    ns, rs_, xs = [], [], []
    for n in range(len(units)):
        for hh in range(2):
            mine = (lane_l < HD_B) if hh == 0 else (lane_l >= HD_B)
            m1 = _dot_bf16(jnp.where(mine, lhs2s[n], 0.0), rhs2s[n], _NT)
            ab_ak = jnp.where(strict, m1[:c], 0.0)
            rs_.append(jnp.where(incl, m1[c:], 0.0))
            ns.append(jnp.where(strict_sq, m1[:c, :c], 0.0))
            xs.append(wy0s[n][:c] + _dot_bf16(jnp.where(second_grp, ab_ak, 0.0),
                                              jnp.concatenate([vps[n], vps[n]], axis=0)))
    xs = [x - _dot_bf16(nm, x) for nm, x in zip(ns, xs)]
    ps = ns
    for _ in range(nsq):
        ps = [_dot_bf16(p, p) for p in ps]
        xs = [x + _dot_bf16(p, x) for p, x in zip(ps, xs)]
    for n, (q, j) in enumerate(units):
        u = jnp.where(first, xs[2 * n], xs[2 * n + 1])
        stack = jnp.concatenate([-u, vps[n]], axis=0)
        y_ref[rsl[n], sls[n]] = wy0s[n][c:] + jnp.where(first, _dot_bf16(rs_[2 * n], stack),
                                                        _dot_bf16(rs_[2 * n + 1], stack))
        ds = _dot_bf16(jnp.concatenate([vps[n], -u], axis=0),
                       jnp.concatenate([kh[rsl[n], sls[n]], bh[rsl[n], sls[n]]], axis=0), _TN)
        s_scr[q * npair + j] = s2s[n] * g_last3[q][:, sls[n]] + jnp.where(blockdiag, ds, 0.0)

    @pl.when(ci == pl.num_programs(1) - 1)
    def _():
        for q in range(nseq):
            for j in range(npair):
                s2 = s_scr[q * npair + j]
                sf_ref[q, 2 * j] = s2[0:HD_B, 0:HD_B]
                sf_ref[q, 2 * j + 1] = s2[HD_B:, HD_B:]


def _wkv_chunked(r, lw, k, v, kk, b, s0, seq_len, c, nseq=1):
    bsz = s0.shape[0]
    nc = seq_len // c
    assert nseq == 1 or nc == 1
    tile = pl.BlockSpec((nseq * c, D_B), lambda bi, ci: (bi * nc + ci, 0))
    st = pl.BlockSpec((nseq, H_B, HD_B, HD_B), lambda bi, ci: (bi, 0, 0, 0))
    return pl.pallas_call(
        functools.partial(_wkv_chunk_kernel, c=c, nseq=nseq),
        grid=(bsz // nseq, nc),
        in_specs=[tile] * 6 + [st],
        out_specs=[tile, st],
        out_shape=[jax.ShapeDtypeStruct((bsz * seq_len, D_B), F32),
                   jax.ShapeDtypeStruct((bsz, H_B, HD_B, HD_B), F32)],
        scratch_shapes=[pltpu.VMEM((nseq * H_B // 2, LANES, LANES), F32)],
        compiler_params=_params(("parallel", "arbitrary")),
        name="wkv_chunked",
    )(r, lw, k, v, kk, b, s0)


def _post_b_kernel(y_ref, r_ref, k_ref, v_ref, g_ref, gain_ref, bias_ref, rk_ref, o_ref):
    y = y_ref[...]
    d = y - _head_sum(y) * (1.0 / HD_B)
    var = _head_sum(d * d) * (1.0 / HD_B)
    yn = d * lax.rsqrt(var + GN_EPS_B) * gain_ref[...] + bias_ref[...]
    v = v_ref[...]
    bonus = _head_sum(r_ref[...] * k_ref[...] * rk_ref[...]) * v
    o_ref[...] = ((yn + bonus) * g_ref[...]).astype(o_ref.dtype)


def _post_b(y, r, k, v, g, gain, bias, r_k, tm):
    m = y.shape[0]
    tile = pl.BlockSpec((tm, D_B), lambda i: (i, 0))
    row = pl.BlockSpec((1, D_B), lambda i: (0, 0))
    return pl.pallas_call(
        _post_b_kernel,
        grid=(m // tm,),
        in_specs=[tile] * 5 + [row] * 3,
        out_specs=tile,
        out_shape=jax.ShapeDtypeStruct((m, D_B), BF16),
        compiler_params=_params(("parallel",)),
        name="rwkv_post",
    )(y, r, k, v, g, gain, bias, r_k)


def _lambda(lam_ref, lam_init):
    l4 = lam_ref[...]
    s1 = jnp.sum(l4[0:1, :] * l4[1:2, :], axis=1, keepdims=True)
    s2 = jnp.sum(l4[2:3, :] * l4[3:4, :], axis=1, keepdims=True)
    return jnp.exp(s1) - jnp.exp(s2) + lam_init


def _diff_combine(o0, o1, lam, subw, lam_init):
    o = o0 - lam * o1
    o = o * lax.rsqrt(jnp.mean(o * o, axis=-1, keepdims=True) + SUBLN_EPS) * subw
    return o * (1.0 - lam_init)


def _flash_kernel(q_ref, k_ref, v_ref, lam_ref, subw_ref, o_ref, m_scr, l_scr, acc_scr, *, tq, tk, lam_init):
    qi = pl.program_id(1)
    m_scr[...] = jnp.full_like(m_scr, NEG_BIG)
    l_scr[...] = jnp.zeros_like(l_scr)
    acc_scr[...] = jnp.zeros_like(acc_scr)

    def block(off, diag):
        kb = k_ref[pl.ds(off, tk), :]
        vb = v_ref[pl.ds(off, tk), :]
        if diag is not None:
            row = lax.broadcasted_iota(jnp.int32, (tq, tk), 0)
            col = lax.broadcasted_iota(jnp.int32, (tq, tk), 1) + diag * tk
            keep = col <= row
        for i in range(2):
            sl = slice(i * HD_C, (i + 1) * HD_C)
            s = lax.dot_general(q_ref[:, sl], kb[:, sl], (((1,), (1,)), ((), ())),
                                preferred_element_type=F32)
            if diag is not None:
                s = jnp.where(keep, s, NEG_BIG)
            m_old = m_scr[i]
            m_new = jnp.maximum(m_old, jnp.max(s, axis=1, keepdims=True))
            corr = jnp.exp2(m_old - m_new)
            p = jnp.exp2(s - jnp.concatenate([m_new] * (tk // LANES), axis=1))
            l_scr[i] = l_scr[i] * corr + jnp.sum(p, axis=1, keepdims=True)
            acc_scr[i] = (acc_scr[i] * jnp.concatenate([corr] * (DV_C // LANES), axis=1)
                          + jnp.dot(p.astype(BF16), vb, preferred_element_type=F32))
            m_scr[i] = m_new

    def full_block(j, carry):
        block(pl.multiple_of(j * tk, tk), None)
        return carry

    lax.fori_loop(0, (qi * tq) // tk, full_block, 0)
    for d in range(tq // tk):
        block(pl.multiple_of(qi * tq + d * tk, tk), d)

    lam = _lambda(lam_ref, lam_init)
    o0 = acc_scr[0] / jnp.concatenate([l_scr[0]] * (DV_C // LANES), axis=1)
    o1 = acc_scr[1] / jnp.concatenate([l_scr[1]] * (DV_C // LANES), axis=1)
    o_ref[...] = _diff_combine(o0, o1, lam, subw_ref[...], lam_init).astype(o_ref.dtype)


def _flash_prompt(q, k, v, lam4, subw, lam_init, tq, tk):
    s = q.shape[0]
    blk = 2 * HD_C
    return pl.pallas_call(
        functools.partial(_flash_kernel, tq=tq, tk=tk, lam_init=lam_init),
        grid=(H_C, s // tq),
        in_specs=[pl.BlockSpec((tq, blk), lambda h, i: (i, h)),
                  pl.BlockSpec((s, blk), lambda h, i: (0, h)),
                  pl.BlockSpec((s, DV_C), lambda h, i: (0, h)),
                  pl.BlockSpec((4, HD_C), lambda h, i: (0, 0)),
                  pl.BlockSpec((1, DV_C), lambda h, i: (0, 0))],
        out_specs=pl.BlockSpec((tq, DV_C), lambda h, i: (i, h)),
        out_shape=jax.ShapeDtypeStruct((s, D_V_C), BF16),
        scratch_shapes=[pltpu.VMEM((2, tq, LANES), F32), pltpu.VMEM((2, tq, LANES), F32),
                        pltpu.VMEM((2, tq, DV_C), F32)],
        compiler_params=_params(("parallel", "arbitrary")),
        name="diff_attn_prompt",
    )(q, k, v, lam4, subw)


DECODE_PAGES = 4


def _decode_kernel(pt_ref, q_ref, kn_ref, vn_ref, *rest, steps, lam_init):
    del pt_ref
    kc_refs, vc_refs = rest[:DECODE_PAGES], rest[DECODE_PAGES:2 * DECODE_PAGES]
    lam_ref, subw_ref, o_ref, qx_scr, qrow_scr, m_scr, l_scr, acc_scr = rest[2 * DECODE_PAGES:]
    pj = pl.program_id(1)
    nrow = 2 * H_C * steps
    grp = 2 * steps
    npseudo = 2 * PAGE_SIZE
    half = H_C * HD_C

    @pl.when(pj == 0)
    def _():
        q = jnp.tile(q_ref[...], (2 * H_C, 1))
        r = lax.broadcasted_iota(jnp.int32, q.shape, 0) // steps
        c = lax.broadcasted_iota(jnp.int32, q.shape, 1) // HD_C
        qrow_scr[...] = jnp.where(r == c, q, 0.0).astype(BF16)
        rx = lax.broadcasted_iota(jnp.int32, (nrow, half), 0) // steps
        cx = lax.broadcasted_iota(jnp.int32, (nrow, half), 1) // HD_C
        folded = jnp.where(rx < H_C, q[:, :half], q[:, half:])
        qx_scr[...] = jnp.where(rx % H_C == cx, folded, 0.0).astype(BF16)
        m_scr[...] = jnp.full_like(m_scr, NEG_BIG)
        l_scr[...] = jnp.zeros_like(l_scr)
        acc_scr[...] = jnp.zeros_like(acc_scr)

    def softmax_step(s):
        m_old = m_scr[...]
        m_new = jnp.maximum(m_old, jnp.max(s, axis=1, keepdims=True))
        corr = jnp.exp(m_old - m_new)
        p = jnp.exp(s - m_new)
        l_scr[...] = l_scr[...] * corr + jnp.sum(p, axis=1, keepdims=True)
        m_scr[...] = m_new
        return p, corr

    kps = jnp.concatenate(
        [jnp.concatenate([ref[pl.ds(h8, npseudo, stride=H_C), :].astype(BF16) for h8 in range(H_C)], axis=1)
         for ref in kc_refs], axis=0)
    s = lax.dot_general(qx_scr[...], kps, (((1,), (1,)), ((), ())), preferred_element_type=F32)
    ncol = s.shape[1]
    col_par = lax.broadcasted_iota(jnp.int32, (nrow, ncol), 1) % 2
    row_par = lax.broadcasted_iota(jnp.int32, (nrow, ncol), 0) // (H_C * steps)
    p, corr = softmax_step(jnp.where(col_par == row_par, s, NEG_BIG))
    for h in range(H_C):
        rows = slice(h * grp, (h + 1) * grp)
        par = (2 * h) // H_C
        ph = p[rows, :]
        other = pltpu.roll(ph, 1 if par == 0 else ncol - 1, 1)
        vps = jnp.concatenate([ref[pl.ds(h, npseudo, stride=H_C), :].astype(BF16) for ref in vc_refs], axis=0)
        pv = jnp.dot(jnp.concatenate([ph, other], axis=0).astype(BF16), vps, preferred_element_type=F32)
        mine, theirs = pv[:grp], pv[grp:]
        both = jnp.concatenate([mine, theirs] if par == 0 else [theirs, mine], axis=1)
        acc_scr[rows, :] = acc_scr[rows, :] * corr[rows, :] + both

    @pl.when(pj == pl.num_programs(1) - 1)
    def _():
        pad = jnp.zeros((PAGE_SIZE - steps, kn_ref.shape[1]), F32)
        kn = jnp.concatenate([kn_ref[...], pad], axis=0).astype(BF16)
        vn = jnp.concatenate([vn_ref[...], pad], axis=0).astype(BF16)
        sn = lax.dot_general(qrow_scr[...], kn, (((1,), (1,)), ((), ())), preferred_element_type=F32)
        step = lax.broadcasted_iota(jnp.int32, (nrow, PAGE_SIZE), 0) % steps
        key = lax.broadcasted_iota(jnp.int32, (nrow, PAGE_SIZE), 1)
        pn, cn = softmax_step(jnp.where(key <= step, sn, NEG_BIG))
        pn = pn.astype(BF16)
        lam = _lambda(lam_ref, lam_init)
        linv = 1.0 / l_scr[...]
        for h in range(H_C):
            rows = slice(h * grp, (h + 1) * grp)
            o = (acc_scr[rows, :] * cn[rows, :] + jnp.dot(pn[rows, :], vn[:, h * DV_C:(h + 1) * DV_C],
                                                          preferred_element_type=F32)) * linv[rows, :]
            o_ref[:, h * DV_C:(h + 1) * DV_C] = _diff_combine(o[:steps], o[steps:], lam, subw_ref[...], lam_init)


def _decode_attn(q, kn, vn, cache_k, cache_v, page_table, lam4, subw, lam_init, steps):
    bsz, n_pages = page_table.shape
    nrow = 2 * H_C * steps
    tile = lambda n: pl.BlockSpec((steps, n), lambda b, p, pt: (b, 0))

    def page(shape, u):
        return pl.BlockSpec((None,) + shape,
                            lambda b, p, pt: (pt[b * n_pages + p * DECODE_PAGES + u],) + (0,) * len(shape))

    grid_spec = pltpu.PrefetchScalarGridSpec(
        num_scalar_prefetch=1,
        grid=(bsz, n_pages // DECODE_PAGES),
        in_specs=([tile(D_QK_C), tile(D_QK_C), tile(D_V_C)]
                  + [page((PAGE_SIZE * 2 * H_C, HD_C), u) for u in range(DECODE_PAGES)]
                  + [page((PAGE_SIZE * H_C * DV_C // LANES, LANES), u) for u in range(DECODE_PAGES)]
                  + [pl.BlockSpec((4, HD_C), lambda b, p, pt: (0, 0)),
                     pl.BlockSpec((1, DV_C), lambda b, p, pt: (0, 0))]),
        out_specs=tile(D_V_C),
        scratch_shapes=[pltpu.VMEM((nrow, H_C * HD_C), BF16), pltpu.VMEM((nrow, D_QK_C), BF16),
                        pltpu.VMEM((nrow, 1), F32), pltpu.VMEM((nrow, 1), F32), pltpu.VMEM((nrow, DV_C), F32)],
    )
    return pl.pallas_call(
        functools.partial(_decode_kernel, steps=steps, lam_init=lam_init),
        grid_spec=grid_spec,
        out_shape=jax.ShapeDtypeStruct((bsz * steps, D_V_C), F32),
        compiler_params=_params(("parallel", "arbitrary")),
        name="diff_attn_decode",
    )(page_table.reshape(-1), q, kn, vn, *([cache_k] * DECODE_PAGES), *([cache_v] * DECODE_PAGES), lam4, subw)


def _ffn_up_short_kernel(h_ref, wg_ref, wv_ref, cwg_ref, cwv_ref, cbg_ref, cbv_ref, bg_ref, bv_ref,
                         o_ref, ng_ref, nv_ref, *, nb, tt):
    c = wg_ref.shape[-1]
    h = h_ref[...]

    def branch(w_ref, cw_ref, cb_ref, buf_ref, new_ref):
        u = jnp.dot(h, w_ref[...], preferred_element_type=F32).reshape(nb, tt, c)
        y = _conv3(u, buf_ref[...], cw_ref[...]) + cb_ref[...]
        new_ref[...] = u[:, tt - 2:tt, :]
        return y.reshape(nb * tt, c)

    gate = branch(wg_ref, cwg_ref, cbg_ref, bg_ref, ng_ref)
    val = branch(wv_ref, cwv_ref, cbv_ref, bv_ref, nv_ref)
    o_ref[...] = (jax.nn.silu(gate) * val).astype(o_ref.dtype)


def _ffn_up_long_kernel(h_ref, wg_ref, wv_ref, cwg_ref, cwv_ref, cbg_ref, cbv_ref, bg_ref, bv_ref,
                        o_ref, ng_ref, nv_ref, cg_scr, cv_scr, *, tm):
    c = wg_ref.shape[-1]

    @pl.when(pl.program_id(1) == 0)
    def _():
        cg_scr[...] = bg_ref[...]
        cv_scr[...] = bv_ref[...]

    h = h_ref[...]

    def branch(w_ref, cw_ref, cb_ref, new_ref, carry_ref):
        u = jnp.dot(h, w_ref[...], preferred_element_type=F32).reshape(1, tm, c)
        y = _conv3(u, carry_ref[...], cw_ref[...]) + cb_ref[...]
        tail = u[:, tm - 2:tm, :]
        new_ref[...] = tail
        carry_ref[...] = tail
        return y.reshape(tm, c)

    gate = branch(wg_ref, cwg_ref, cbg_ref, ng_ref, cg_scr)
    val = branch(wv_ref, cwv_ref, cbv_ref, nv_ref, cv_scr)
    o_ref[...] = (jax.nn.silu(gate) * val).astype(o_ref.dtype)


def _ffn_up(h, w_up, layer, conv_w, conv_b, buf, seq_len, tm, tn):
    m, d = h.shape
    nj = D_FF // tn
    nb, tt = _seq_tiles(tm, seq_len)
    carried = nb == 1
    bsz = buf.shape[0]

    def cols(off):
        return lambda j, i: (0, j + off)

    def wcols(off):
        return lambda j, i: (layer, 0, j + off)

    def buf_spec(off):
        if carried:
            return pl.BlockSpec((1, 2, tn), lambda j, i: (0, 0, j + off))
        return pl.BlockSpec((nb, 2, tn), lambda j, i: (i, 0, j + off))

    if carried:
        body = functools.partial(_ffn_up_long_kernel, tm=tm)
        scratch = [pltpu.VMEM((1, 2, tn), F32), pltpu.VMEM((1, 2, tn), F32)]
    else:
        body = functools.partial(_ffn_up_short_kernel, nb=nb, tt=tt)
        scratch = []
    return pl.pallas_call(
        body,
        grid=(nj, m // tm),
        in_specs=[pl.BlockSpec((tm, d), lambda j, i: (i, 0)),
                  pl.BlockSpec((None, d, tn), wcols(0)), pl.BlockSpec((None, d, tn), wcols(nj)),
                  pl.BlockSpec((CONV_W, tn), cols(0)), pl.BlockSpec((CONV_W, tn), cols(nj)),
                  pl.BlockSpec((1, tn), cols(0)), pl.BlockSpec((1, tn), cols(nj)),
                  buf_spec(0), buf_spec(nj)],
        out_specs=[pl.BlockSpec((tm, tn), lambda j, i: (i, j)), buf_spec(0), buf_spec(0)],
        out_shape=[jax.ShapeDtypeStruct((m, D_FF), BF16),
                   jax.ShapeDtypeStruct((bsz, 2, D_FF), F32), jax.ShapeDtypeStruct((bsz, 2, D_FF), F32)],
        scratch_shapes=scratch,
        compiler_params=_params(("parallel", "arbitrary")),
        name="ffn_up",
    )(h, w_up, w_up, conv_w, conv_w, conv_b, conv_b, buf, buf)


def _pad_cols(x, n):
    return jnp.pad(x, [(0, 0)] * (x.ndim - 1) + [(0, n - x.shape[-1])])


def _trunk(x, conv_a, shift_b, wkv_b, ffn_buf, cache_k, cache_v, page_table, P, seq_len):
    m = x.shape[0]
    bsz = m // seq_len
    tm = 1024
    tm_small = 256

    g0 = P["norm_mix"][0]
    pa = _mm([x], [P["w_in_a"]], [(0, 0)], P_A, tm=tm, tn=1024, norm_gain=g0, name="proj_in_a")
    pb = _mm([x], [P["w_in_b"]], [(0, 0)], P_B_PAD, tm=tm, tn=P_B_PAD // 3, norm_gain=g0, name="proj_in_b")
    y_a, new_conv = _mixer_a(pa, P["conv_w_a"], conv_a, seq_len, 512)
    (r, lw, k2, v, kk, b, g, new_shift) = _prep_b(
        pb, shift_b, P["mu_b"], P["w0_b"], P["a0_b"], P["k_k_b"], P["k_a_b"],
        P["w2_b"], P["a2_b"], P["g2_b"], seq_len, tm_small)
    if seq_len % WKV_CHUNK == 0:
        y, new_wkv = _wkv_chunked(r, lw, k2, v, kk, b, wkv_b, seq_len, WKV_CHUNK)
    else:
        y, new_wkv = _wkv_chunked(r, lw, k2, v, kk, b, wkv_b, seq_len, seq_len, nseq=WKV_SHORT_NSEQ)
    y_b = _post_b(y, r, k2, v, g, P["ln_gain_b"], P["ln_bias_b"], P["r_k_b"], tm_small)
    x = _mm([y_a, y_b], [P["w_out_ab"], P["w_out_ab"]], [(0, 0), (1, 0)], D_MODEL,
            tm=tm, tn=1024, res=x, name="proj_out_ab")

    h = _rms(x, P["norm_ffn"][0], BF16)
    gated, nfg0, nfv0 = _ffn_up(h, P["w_up_f"], 0, P["conv_w_f"][0], P["conv_b_f"][0], ffn_buf[0],
                                seq_len, tm, 512)
    x = _mm([gated], [P["w_down_f"][0]], [(0, 0)], D_MODEL, tm=tm, tn=512, res=x, name="ffn_down")

    lam_init = 0.8 - 0.6 * math.exp(-0.3 * 1)
    g1 = P["norm_mix"][1]
    nq = D_QK_C // 1024
    if cache_k is None:
        qh = _mm([x], [P["w_qkv_c"]], [(0, 0)], D_QK_C, tm=tm, tn=1024, out_dtypes=(BF16,),
                 scale=ATTN_SCALE * math.log2(math.e), norm_gain=g1, name="proj_q")
        k_rows, k_bf = _mm([x], [P["w_qkv_c"]], [(0, nq)], D_QK_C, tm=tm, tn=1024,
                           out_dtypes=(F32, BF16), norm_gain=g1, name="proj_k")
        v_rows, v_bf = _mm([x], [P["w_qkv_c"]], [(0, 2 * nq)], D_V_C, tm=tm, tn=1024,
                           out_dtypes=(F32, BF16), norm_gain=g1, name="proj_v")
        attn = _flash_prompt(qh, k_bf, v_bf, P["lam4"], P["subln_w_c"], lam_init, 1024, 512)
    else:
        qh = _mm([x], [P["w_qkv_c"]], [(0, 0)], D_QK_C, tm=tm, tn=1024, scale=ATTN_SCALE, norm_gain=g1, name="proj_q")
        k_rows = _mm([x], [P["w_qkv_c"]], [(0, nq)], D_QK_C, tm=tm, tn=1024, norm_gain=g1, name="proj_k")
        v_rows = _mm([x], [P["w_qkv_c"]], [(0, 2 * nq)], D_V_C, tm=tm, tn=1024, norm_gain=g1, name="proj_v")
        attn = _decode_attn(qh, k_rows, v_rows, cache_k, cache_v, page_table, P["lam4"], P["subln_w_c"],
                            lam_init, seq_len)
    x = _mm([attn], [P["w_out_c"]], [(0, 0)], D_MODEL, tm=tm, tn=1024, res=x, name="proj_out_c")

    h = _rms(x, P["norm_ffn"][1], BF16)
    gated, nfg1, nfv1 = _ffn_up(h, P["w_up_f"], 1, P["conv_w_f"][1], P["conv_b_f"][1], ffn_buf[1],
                                seq_len, tm, 512)
    x = _mm([gated], [P["w_down_f"][1]], [(0, 0)], D_MODEL, tm=tm, tn=512, res=x, name="ffn_down")

    y_out = _rms(x, P["norm_final"], F32)
    new_ffn = jnp.stack([jnp.concatenate([nfg0, nfv0], axis=-1), jnp.concatenate([nfg1, nfv1], axis=-1)])
    return (y_out.reshape(bsz, seq_len, D_MODEL),
            new_conv[None],
            new_shift[None, :, 0, :P_B],
            new_wkv[None],
            k_rows.reshape(1, bsz, seq_len, 2 * H_C, HD_C),
            v_rows.reshape(1, bsz, seq_len, H_C, DV_C),
            new_ffn)


def kernel(x_prompt, x_sample, state_conv_a, state_shift_b, state_wkv_b, cache_k, cache_v, page_table,
           state_ffn_conv, norm_mix, norm_ffn, norm_final, w_in_ab, conv_w_a, mu_b, w0_b, w2_b, a0_b, a2_b,
           g2_b, k_k_b, k_a_b, r_k_b, ln_gain_b, ln_bias_b, w_out_ab, w_qkv_c, lambda_q1_c, lambda_k1_c,
           lambda_q2_c, lambda_k2_c, subln_w_c, w_out_c, w_up_f, conv_w_f, conv_b_f, w_down_f):
    bp, seq, d = x_prompt.shape
    bs, dec_seq, _ = x_sample.shape
    row = lambda a: a.reshape(1, -1)
    lora_rows = jnp.zeros((LORA_PAD, D_B), F32)
    P = dict(
        norm_mix=norm_mix, norm_ffn=norm_ffn, norm_final=norm_final,
        w_in_a=w_in_ab[0, :, :P_A].astype(BF16),
        w_in_b=_pad_cols(w_in_ab[0, :, P_A:], P_B_PAD).astype(BF16),
        conv_w_a=conv_w_a[0],
        mu_b=_pad_cols(row(mu_b[0]), P_B_PAD),
        w0_b=row(w0_b[0]), a0_b=row(a0_b[0]), k_k_b=row(k_k_b[0]), k_a_b=row(k_a_b[0]),
        w2_b=lora_rows.at[:R_W].set(w2_b[0]).astype(BF16),
        a2_b=lora_rows.at[R_W:R_W + R_A].set(a2_b[0]).astype(BF16),
        g2_b=lora_rows.at[R_W + R_A:R_W + R_A + R_G].set(g2_b[0]).astype(BF16),
        r_k_b=row(r_k_b[0]), ln_gain_b=row(ln_gain_b[0]), ln_bias_b=row(ln_bias_b[0]),
        w_out_ab=w_out_ab[0].astype(BF16),
        w_qkv_c=w_qkv_c[0].astype(BF16),
        lam4=jnp.stack([lambda_q1_c[0], lambda_k1_c[0], lambda_q2_c[0], lambda_k2_c[0]]),
        subln_w_c=row(subln_w_c[0]),
        w_out_c=w_out_c[0].astype(BF16),
        w_up_f=w_up_f.astype(BF16), conv_w_f=conv_w_f, conv_b_f=conv_b_f[:, None, :],
        w_down_f=w_down_f.astype(BF16),
    )
    n_phys = cache_k.shape[1]
    outs_p = _trunk(
        x_prompt.reshape(bp * seq, d),
        jnp.zeros((bp, CONV_W - 1, D_A), F32),
        jnp.zeros((bp, 1, P_B_PAD), F32),
        jnp.zeros((bp, H_B, HD_B, HD_B), F32),
        jnp.zeros((DEPTH, bp, CONV_W - 1, 2 * D_FF), F32),
        None, None, None, P, seq)
    outs_s = _trunk(
        x_sample.reshape(bs * dec_seq, d),
        state_conv_a[0],
        _pad_cols(state_shift_b[0], P_B_PAD)[:, None, :],
        state_wkv_b[0],
        state_ffn_conv,
        cache_k.reshape(n_phys, PAGE_SIZE * 2 * H_C, HD_C),
        cache_v.reshape(n_phys, PAGE_SIZE, H_C, DV_C // LANES, LANES).swapaxes(2, 3).reshape(
            n_phys, PAGE_SIZE * H_C * DV_C // LANES, LANES),
        page_table, P, dec_seq)
    return (outs_p[0], outs_s[0]) + outs_p[1:] + outs_s[1:]
```

```python
import functools
import math

import jax
import jax.numpy as jnp
from jax import lax
from jax.experimental import pallas as pl
from jax.experimental.pallas import tpu as pltpu

F32 = jnp.float32
BF16 = jnp.bfloat16

D_MODEL = 2048
DEPTH = 2
PAGE_SIZE = 128
CONV_W = 3
NORM_EPS = 1e-6
D_A = D_MODEL // 2
P_A = 3 * D_A
D_B = D_MODEL // 2
HD_B = 64
H_B = D_B // HD_B
R_W = 64
R_A = 64
R_G = 160
P_B = 3 * D_B + R_W + R_A + R_G
LORA_PAD = 384
P_B_PAD = 3 * D_B + LORA_PAD
GN_EPS_B = 64e-5
HD_C = 128
H_C = D_MODEL // (2 * HD_C)
DV_C = 2 * HD_C
D_QK_C = 2 * H_C * HD_C
D_V_C = H_C * DV_C
ATTN_SCALE = HD_C ** -0.5
SUBLN_EPS = 1e-5
D_FF = 5632

LANES = 128
SUBLANES = 8
VMEM_LIMIT = 52 * 1024 * 1024
NEG_BIG = -1e30


def _params(sem):
    return pltpu.CompilerParams(dimension_semantics=sem, vmem_limit_bytes=VMEM_LIMIT)


def _rms_kernel(x_ref, g_ref, o_ref):
    x = x_ref[...]
    y = x * lax.rsqrt(jnp.mean(x * x, axis=-1, keepdims=True) + NORM_EPS)
    o_ref[...] = (y * g_ref[...]).astype(o_ref.dtype)


def _rms(x, g, out_dtype, tm=512):
    m, d = x.shape
    return pl.pallas_call(
        _rms_kernel,
        grid=(m // tm,),
        in_specs=[pl.BlockSpec((tm, d), lambda i: (i, 0)), pl.BlockSpec((1, d), lambda i: (0, 0))],
        out_specs=pl.BlockSpec((tm, d), lambda i: (i, 0)),
        out_shape=jax.ShapeDtypeStruct((m, d), out_dtype),
        compiler_params=_params(("parallel",)),
        name="rmsnorm",
    )(x, g.reshape(1, d))


def _mm_kernel(*refs, nx, has_res, scale):
    xs, ws = refs[:nx], refs[nx:2 * nx]
    res = refs[2 * nx] if has_res else None
    outs = refs[2 * nx + int(has_res):]
    acc = None
    for x, w in zip(xs, ws):
        d = jnp.dot(x[...].astype(BF16), w[...], preferred_element_type=F32)
        acc = d if acc is None else acc + d
    if scale is not None:
        acc = acc * scale
    if res is not None:
        acc = acc + res[...]
    for o in outs:
        o[...] = acc.astype(o.dtype)


def _mm(xs, ws, wblocks, n, *, tm, tn, out_dtypes=(F32,), res=None, scale=None, name="matmul"):
    m = xs[0].shape[0]
    in_specs, args = [], []
    for x in xs:
        in_specs.append(pl.BlockSpec((tm, x.shape[1]), lambda i, j: (i, 0)))
        args.append(x)
    for x, w, (rb, cb) in zip(xs, ws, wblocks):
        in_specs.append(pl.BlockSpec((x.shape[1], tn), lambda i, j, rb=rb, cb=cb: (rb, cb + j)))
        args.append(w)
    if res is not None:
        in_specs.append(pl.BlockSpec((tm, tn), lambda i, j: (i, j)))
        args.append(res)
    outs = pl.pallas_call(
        functools.partial(_mm_kernel, nx=len(xs), has_res=res is not None, scale=scale),
        grid=(m // tm, n // tn),
        in_specs=in_specs,
        out_specs=[pl.BlockSpec((tm, tn), lambda i, j: (i, j)) for _ in out_dtypes],
        out_shape=[jax.ShapeDtypeStruct((m, n), dt) for dt in out_dtypes],
        compiler_params=_params(("parallel", "arbitrary")),
        name=name,
    )(*args)
    return outs if len(out_dtypes) > 1 else outs[0]


def _seq_tiles(tm, seq_len):
    return (tm // seq_len, seq_len) if seq_len < tm else (1, tm)


def _shift_rows(u3, prev, k):
    out = pltpu.roll(u3, k, 1)
    head = out[:, :SUBLANES, :]
    t = lax.broadcasted_iota(jnp.int32, head.shape, 1)
    for s in range(k):
        head = jnp.where(t == s, prev[:, 2 - k + s:3 - k + s, :], head)
    if u3.shape[1] == SUBLANES:
        return head
    return jnp.concatenate([head, out[:, SUBLANES:, :]], axis=1)


def _conv3(u3, prev, w):
    return w[0:1, :] * _shift_rows(u3, prev, 2) + w[1:2, :] * _shift_rows(u3, prev, 1) + w[2:3, :] * u3


def _mixer_a_kernel(gb_ref, gc_ref, xt_ref, w_ref, buf_ref, ya_ref, nc_ref, carry_ref, *, nb, tt, carried):
    c = gb_ref.shape[-1]
    cx = (gc_ref[...] * xt_ref[...]).reshape(nb, tt, c)
    if carried:
        @pl.when(pl.program_id(0) == 0)
        def _():
            carry_ref[...] = buf_ref[...]
        prev = carry_ref[...]
    else:
        prev = buf_ref[...]
    conv = _conv3(cx, prev, w_ref[...])
    ya_ref[...] = (gb_ref[...] * conv.reshape(nb * tt, c)).astype(ya_ref.dtype)
    tail = cx[:, tt - 2:tt, :]
    nc_ref[...] = tail
    if carried:
        carry_ref[...] = tail


def _mixer_a(pa, conv_w, buf, seq_len, tm):
    m = pa.shape[0]
    nb, tt = _seq_tiles(tm, seq_len)
    carried = nb == 1
    bsz = buf.shape[0]
    buf_spec = (pl.BlockSpec((1, 2, D_A), lambda i: (0, 0, 0)) if carried
                else pl.BlockSpec((nb, 2, D_A), lambda i: (i, 0, 0)))
    ya, nc = pl.pallas_call(
        functools.partial(_mixer_a_kernel, nb=nb, tt=tt, carried=carried),
        grid=(m // tm,),
        in_specs=[pl.BlockSpec((tm, D_A), lambda i: (i, 0)),
                  pl.BlockSpec((tm, D_A), lambda i: (i, 1)),
                  pl.BlockSpec((tm, D_A), lambda i: (i, 2)),
                  pl.BlockSpec((CONV_W, D_A), lambda i: (0, 0)),
                  buf_spec],
        out_specs=[pl.BlockSpec((tm, D_A), lambda i: (i, 0)), buf_spec],
        out_shape=[jax.ShapeDtypeStruct((m, D_A), BF16), jax.ShapeDtypeStruct((bsz, 2, D_A), F32)],
        scratch_shapes=[pltpu.VMEM((1, 2, D_A), F32)],
        compiler_params=_params(("arbitrary",)),
        name="mixer_a",
    )(pa, pa, pa, conv_w, buf)
    return ya, nc


def _split_bf16(x):
    hi = x.astype(BF16)
    lo = (x - hi.astype(F32)).astype(BF16)
    return hi, lo


def _head_sum(x):
    blk = 2 * LANES
    r = lax.broadcasted_iota(jnp.int32, (blk, blk), 0) // HD_B
    c = lax.broadcasted_iota(jnp.int32, (blk, blk), 1) // HD_B
    ones = jnp.where(r == c, 1.0, 0.0).astype(BF16)
    hi, lo = _split_bf16(x)
    parts = []
    for s in range(x.shape[1] // blk):
        sl = slice(s * blk, (s + 1) * blk)
        parts.append(jnp.dot(hi[:, sl], ones, preferred_element_type=F32)
                     + jnp.dot(lo[:, sl], ones, preferred_element_type=F32))
    return jnp.concatenate(parts, axis=1)


def _prep_b_kernel(pb_ref, prev_ref, mu_ref, w0_ref, a0_ref, kk_ref, ka_ref, w2_ref, a2_ref, g2_ref,
                   r_out, lw_out, k_out, v_out, kk_out, b_out, g_out, ns_out,
                   carry_ref, *, nb, tt, carried):
    c = pb_ref.shape[-1]
    pb3 = pb_ref[...].reshape(nb, tt, c)
    if carried:
        @pl.when(pl.program_id(0) == 0)
        def _():
            carry_ref[...] = prev_ref[...]
        prev = carry_ref[...]
    else:
        prev = prev_ref[...]
    t = lax.broadcasted_iota(jnp.int32, pb3.shape, 1)
    shifted = jnp.where(t == 0, prev, pltpu.roll(pb3, 1, 1))
    last = pb3[:, tt - 1:tt, :]
    ns_out[...] = last
    if carried:
        carry_ref[...] = last
    x = (pb3 + mu_ref[...] * (shifted - pb3)).reshape(nb * tt, c)
    r = x[:, :D_B]
    k = x[:, D_B:2 * D_B]
    v = x[:, 2 * D_B:3 * D_B]
    lo = x[:, 3 * D_B:]
    wl = w0_ref[...] + jnp.dot(jnp.tanh(lo).astype(BF16), w2_ref[...], preferred_element_type=F32)
    softplus_neg = jnp.maximum(-wl, 0.0) + jnp.log(1.0 + jnp.exp(-jnp.abs(wl)))
    w_log = -softplus_neg - 0.5
    a = jax.nn.sigmoid(a0_ref[...] + jnp.dot(lo.astype(BF16), a2_ref[...], preferred_element_type=F32))
    g = jnp.dot(jax.nn.sigmoid(lo).astype(BF16), g2_ref[...], preferred_element_type=F32)
    kk = k * kk_ref[...]
    kk = kk / jnp.maximum(jnp.sqrt(_head_sum(kk * kk)), 1e-12)
    k2 = k * (1.0 + (a - 1.0) * ka_ref[...])
    b = kk * a
    r_out[...] = r
    lw_out[...] = -jnp.exp(w_log)
    k_out[...] = k2
    v_out[...] = v
    kk_out[...] = kk
    b_out[...] = b
    g_out[...] = g


def _prep_b(pb, shift_prev, mu, w0, a0, k_k, k_a, w2p, a2p, g2p, seq_len, tm):
    m = pb.shape[0]
    nb, tt = _seq_tiles(tm, seq_len)
    carried = nb == 1
    bsz = shift_prev.shape[0]
    prev_spec = (pl.BlockSpec((1, 1, P_B_PAD), lambda i: (0, 0, 0)) if carried
                 else pl.BlockSpec((nb, 1, P_B_PAD), lambda i: (i, 0, 0)))
    row = lambda n: pl.BlockSpec((1, n), lambda i: (0, 0))
    lora = pl.BlockSpec((LORA_PAD, D_B), lambda i: (0, 0))
    tile = pl.BlockSpec((tm, D_B), lambda i: (i, 0))
    outs = pl.pallas_call(
        functools.partial(_prep_b_kernel, nb=nb, tt=tt, carried=carried),
        grid=(m // tm,),
        in_specs=[pl.BlockSpec((tm, P_B_PAD), lambda i: (i, 0)), prev_spec, row(P_B_PAD),
                  row(D_B), row(D_B), row(D_B), row(D_B), lora, lora, lora],
        out_specs=[tile] * 7 + [prev_spec],
        out_shape=[jax.ShapeDtypeStruct((m, D_B), F32)] * 7 + [jax.ShapeDtypeStruct((bsz, 1, P_B_PAD), F32)],
        scratch_shapes=[pltpu.VMEM((1, 1, P_B_PAD), F32)],
        compiler_params=_params(("arbitrary",)),
        name="rwkv_prep",
    )(pb, shift_prev, mu, w0, a0, k_k, k_a, w2p, a2p, g2p)
    return outs


def _split3_bf16(x):
    hi = x.astype(BF16)
    rest = x - hi.astype(F32)
    mid = rest.astype(BF16)
    return hi, mid, (rest - mid.astype(F32)).astype(BF16)


def _dot_bf16(a, b, dims=(((1,), (0,)), ((), ()))):
    return lax.dot_general(a.astype(BF16), b.astype(BF16), dims, preferred_element_type=F32)


_NT = (((1,), (1,)), ((), ()))
_TN = (((0,), (0,)), ((), ()))
WKV_CHUNK = 128
WKV_SHORT_NSEQ = 8


def _wkv_chunk_kernel(r_ref, lw_ref, k_ref, v_ref, kk_ref, b_ref, s0_ref, y_ref, sf_ref, s_scr, *, c, nseq):
    ci = pl.program_id(1)
    npair = H_B // 2
    rows = nseq * c

    @pl.when(ci == 0)
    def _():
        s_scr[...] = jnp.zeros_like(s_scr)
        for q in range(nseq):
            for j in range(npair):
                s_scr[q * npair + j, 0:HD_B, 0:HD_B] = s0_ref[q, 2 * j]
                s_scr[q * npair + j, HD_B:, HD_B:] = s0_ref[q, 2 * j + 1]

    ti = lax.broadcasted_iota(jnp.int32, (rows, rows), 0)
    tj = lax.broadcasted_iota(jnp.int32, (rows, rows), 1)
    tri = jnp.where(jnp.logical_and(ti // c == tj // c, tj <= ti), 1.0, 0.0).astype(BF16)
    lw = lw_ref[...]
    cum = None
    for limb in _split3_bf16(lw):
        d = jnp.dot(tri, limb, preferred_element_type=F32)
        cum = d if cum is None else cum + d
    cum3 = cum.reshape(nseq, c, D_B)
    last3 = cum3[:, c - 1:c, :]
    mid3 = cum3[:, c // 2 - 1:c // 2, :]
    cc = (cum3 - mid3).reshape(rows, D_B)
    g_mid3 = jnp.exp(mid3)
    g_last3 = jnp.exp(last3)
    g_inv = jnp.exp(-cc)
    kt = kk_ref[...] * jnp.exp(cc - lw)
    rt = r_ref[...] * jnp.exp(cc)
    bt = b_ref[...] * g_inv
    kq = k_ref[...] * g_inv
    to_end = jnp.exp(last3 - cum3).reshape(rows, D_B)
    kh = k_ref[...] * to_end
    bh = b_ref[...] * to_end
    v = v_ref[...]

    ci_ = lax.broadcasted_iota(jnp.int32, (c, c), 0)
    cj_ = lax.broadcasted_iota(jnp.int32, (c, c), 1)
    first = lax.broadcasted_iota(jnp.int32, (c, LANES), 1) < HD_B
    col2 = lax.broadcasted_iota(jnp.int32, (c, 2 * c), 1)
    row2 = lax.broadcasted_iota(jnp.int32, (c, 2 * c), 0)
    strict = (col2 % c) < row2
    incl = (col2 % c) <= row2
    second_grp = col2 >= c
    strict_sq = cj_ < ci_
    blockdiag = (lax.broadcasted_iota(jnp.int32, (LANES, LANES), 0) // HD_B
                 == lax.broadcasted_iota(jnp.int32, (LANES, LANES), 1) // HD_B)
    lane_l = lax.broadcasted_iota(jnp.int32, (2 * c, LANES), 1)
    nsq = (c - 1).bit_length() - 1

    units = [(q, j) for q in range(nseq) for j in range(npair)]
    rsl = [slice(q * c, (q + 1) * c) for q, j in units]
    sls = [slice(j * LANES, (j + 1) * LANES) for q, j in units]
    s2s = [s_scr[q * npair + j] for q, j in units]
    lhs2s = [jnp.concatenate([kt[rs, sl], rt[rs, sl]], axis=0) for rs, sl in zip(rsl, sls)]
    rhs2s = [jnp.concatenate([bt[rs, sl], kq[rs, sl]], axis=0) for rs, sl in zip(rsl, sls)]
    vps = [v[rs, sl] for rs, sl in zip(rsl, sls)]
    wy0s = [_dot_bf16(lhs2s[n], s2s[n] * g_mid3[q][:, sls[n]], _NT)
            for n, (q, j) in enumerate(units)]
    ns, rs_, xs = [], [], []
    for n in range(len(units)):
        for hh in range(2):
            mine = (lane_l < HD_B) if hh == 0 else (lane_l >= HD_B)
            m1 = _dot_bf16(jnp.where(mine, lhs2s[n], 0.0), rhs2s[n], _NT)
            ab_ak = jnp.where(strict, m1[:c], 0.0)
            rs_.append(jnp.where(incl, m1[c:], 0.0))
            ns.append(jnp.where(strict_sq, m1[:c, :c], 0.0))
            xs.append(wy0s[n][:c] + _dot_bf16(jnp.where(second_grp, ab_ak, 0.0),
                                              jnp.concatenate([vps[n], vps[n]], axis=0)))
    xs = [x - _dot_bf16(nm, x) for nm, x in zip(ns, xs)]
    ps = ns
    for _ in range(nsq):
        ps = [_dot_bf16(p, p) for p in ps]
        xs = [x + _dot_bf16(p, x) for p, x in zip(ps, xs)]
    for n, (q, j) in enumerate(units):
        u = jnp.where(first, xs[2 * n], xs[2 * n + 1])
        stack = jnp.concatenate([-u, vps[n]], axis=0)
        y_ref[rsl[n], sls[n]] = wy0s[n][c:] + jnp.where(first, _dot_bf16(rs_[2 * n], stack),
                                                        _dot_bf16(rs_[2 * n + 1], stack))
        ds = _dot_bf16(jnp.concatenate([vps[n], -u], axis=0),
                       jnp.concatenate([kh[rsl[n], sls[n]], bh[rsl[n], sls[n]]], axis=0), _TN)
        s_scr[q * npair + j] = s2s[n] * g_last3[q][:, sls[n]] + jnp.where(blockdiag, ds, 0.0)

    @pl.when(ci == pl.num_programs(1) - 1)
    def _():
        for q in range(nseq):
            for j in range(npair):
                s2 = s_scr[q * npair + j]
                sf_ref[q, 2 * j] = s2[0:HD_B, 0:HD_B]
                sf_ref[q, 2 * j + 1] = s2[HD_B:, HD_B:]


def _wkv_chunked(r, lw, k, v, kk, b, s0, seq_len, c, nseq=1):
    bsz = s0.shape[0]
    nc = seq_len // c
    assert nseq == 1 or nc == 1
    tile = pl.BlockSpec((nseq * c, D_B), lambda bi, ci: (bi * nc + ci, 0))
    st = pl.BlockSpec((nseq, H_B, HD_B, HD_B), lambda bi, ci: (bi, 0, 0, 0))
    return pl.pallas_call(
        functools.partial(_wkv_chunk_kernel, c=c, nseq=nseq),
        grid=(bsz // nseq, nc),
        in_specs=[tile] * 6 + [st],
        out_specs=[tile, st],
        out_shape=[jax.ShapeDtypeStruct((bsz * seq_len, D_B), F32),
                   jax.ShapeDtypeStruct((bsz, H_B, HD_B, HD_B), F32)],
        scratch_shapes=[pltpu.VMEM((nseq * H_B // 2, LANES, LANES), F32)],
        compiler_params=_params(("parallel", "arbitrary")),
        name="wkv_chunked",
    )(r, lw, k, v, kk, b, s0)


def _post_b_kernel(y_ref, r_ref, k_ref, v_ref, g_ref, gain_ref, bias_ref, rk_ref, o_ref):
    y = y_ref[...]
    d = y - _head_sum(y) * (1.0 / HD_B)
    var = _head_sum(d * d) * (1.0 / HD_B)
    yn = d * lax.rsqrt(var + GN_EPS_B) * gain_ref[...] + bias_ref[...]
    v = v_ref[...]
    bonus = _head_sum(r_ref[...] * k_ref[...] * rk_ref[...]) * v
    o_ref[...] = ((yn + bonus) * g_ref[...]).astype(o_ref.dtype)


def _post_b(y, r, k, v, g, gain, bias, r_k, tm):
    m = y.shape[0]
    tile = pl.BlockSpec((tm, D_B), lambda i: (i, 0))
    row = pl.BlockSpec((1, D_B), lambda i: (0, 0))
    return pl.pallas_call(
        _post_b_kernel,
        grid=(m // tm,),
        in_specs=[tile] * 5 + [row] * 3,
        out_specs=tile,
        out_shape=jax.ShapeDtypeStruct((m, D_B), BF16),
        compiler_params=_params(("parallel",)),
        name="rwkv_post",
    )(y, r, k, v, g, gain, bias, r_k)


def _lambda(lam_ref, lam_init):
    l4 = lam_ref[...]
    s1 = jnp.sum(l4[0:1, :] * l4[1:2, :], axis=1, keepdims=True)
    s2 = jnp.sum(l4[2:3, :] * l4[3:4, :], axis=1, keepdims=True)
    return jnp.exp(s1) - jnp.exp(s2) + lam_init


def _diff_combine(o0, o1, lam, subw, lam_init):
    o = o0 - lam * o1
    o = o * lax.rsqrt(jnp.mean(o * o, axis=-1, keepdims=True) + SUBLN_EPS) * subw
    return o * (1.0 - lam_init)


def _flash_kernel(q_ref, k_ref, v_ref, lam_ref, subw_ref, o_ref, m_scr, l_scr, acc_scr, *, tq, tk, lam_init):
    qi = pl.program_id(1)
    m_scr[...] = jnp.full_like(m_scr, NEG_BIG)
    l_scr[...] = jnp.zeros_like(l_scr)
    acc_scr[...] = jnp.zeros_like(acc_scr)

    def block(off, diag):
        kb = k_ref[pl.ds(off, tk), :]
        vb = v_ref[pl.ds(off, tk), :]
        if diag is not None:
            row = lax.broadcasted_iota(jnp.int32, (tq, tk), 0)
            col = lax.broadcasted_iota(jnp.int32, (tq, tk), 1) + diag * tk
            keep = col <= row
        for i in range(2):
            sl = slice(i * HD_C, (i + 1) * HD_C)
            s = lax.dot_general(q_ref[:, sl], kb[:, sl], (((1,), (1,)), ((), ())),
                                preferred_element_type=F32)
            if diag is not None:
                s = jnp.where(keep, s, NEG_BIG)
            m_old = m_scr[i]
            m_new = jnp.maximum(m_old, jnp.max(s, axis=1, keepdims=True))
            corr = jnp.exp2(m_old - m_new)
            p = jnp.exp2(s - jnp.concatenate([m_new] * (tk // LANES), axis=1))
            l_scr[i] = l_scr[i] * corr + jnp.sum(p, axis=1, keepdims=True)
            acc_scr[i] = (acc_scr[i] * jnp.concatenate([corr] * (DV_C // LANES), axis=1)
                          + jnp.dot(p.astype(BF16), vb, preferred_element_type=F32))
            m_scr[i] = m_new

    def full_block(j, carry):
        block(pl.multiple_of(j * tk, tk), None)
        return carry

    lax.fori_loop(0, (qi * tq) // tk, full_block, 0)
    for d in range(tq // tk):
        block(pl.multiple_of(qi * tq + d * tk, tk), d)

    lam = _lambda(lam_ref, lam_init)
    o0 = acc_scr[0] / jnp.concatenate([l_scr[0]] * (DV_C // LANES), axis=1)
    o1 = acc_scr[1] / jnp.concatenate([l_scr[1]] * (DV_C // LANES), axis=1)
    o_ref[...] = _diff_combine(o0, o1, lam, subw_ref[...], lam_init).astype(o_ref.dtype)


def _flash_prompt(q, k, v, lam4, subw, lam_init, tq, tk):
    s = q.shape[0]
    blk = 2 * HD_C
    return pl.pallas_call(
        functools.partial(_flash_kernel, tq=tq, tk=tk, lam_init=lam_init),
        grid=(H_C, s // tq),
        in_specs=[pl.BlockSpec((tq, blk), lambda h, i: (i, h)),
                  pl.BlockSpec((s, blk), lambda h, i: (0, h)),
                  pl.BlockSpec((s, DV_C), lambda h, i: (0, h)),
                  pl.BlockSpec((4, HD_C), lambda h, i: (0, 0)),
                  pl.BlockSpec((1, DV_C), lambda h, i: (0, 0))],
        out_specs=pl.BlockSpec((tq, DV_C), lambda h, i: (i, h)),
        out_shape=jax.ShapeDtypeStruct((s, D_V_C), BF16),
        scratch_shapes=[pltpu.VMEM((2, tq, LANES), F32), pltpu.VMEM((2, tq, LANES), F32),
                        pltpu.VMEM((2, tq, DV_C), F32)],
        compiler_params=_params(("parallel", "arbitrary")),
        name="diff_attn_prompt",
    )(q, k, v, lam4, subw)


DECODE_PAGES = 8


def _decode_kernel(pt_ref, q_ref, kn_ref, vn_ref, *rest, steps, lam_init):
    del pt_ref
    kc_refs, vc_refs = rest[:DECODE_PAGES], rest[DECODE_PAGES:2 * DECODE_PAGES]
    lam_ref, subw_ref, o_ref, qx_scr, qrow_scr, m_scr, l_scr, acc_scr = rest[2 * DECODE_PAGES:]
    pj = pl.program_id(1)
    nrow = 2 * H_C * steps
    grp = 2 * steps
    npseudo = 2 * PAGE_SIZE
    half = H_C * HD_C

    @pl.when(pj == 0)
    def _():
        q = jnp.tile(q_ref[...], (2 * H_C, 1))
        r = lax.broadcasted_iota(jnp.int32, q.shape, 0) // steps
        c = lax.broadcasted_iota(jnp.int32, q.shape, 1) // HD_C
        qrow_scr[...] = jnp.where(r == c, q, 0.0).astype(BF16)
        rx = lax.broadcasted_iota(jnp.int32, (nrow, half), 0) // steps
        cx = lax.broadcasted_iota(jnp.int32, (nrow, half), 1) // HD_C
        folded = jnp.where(rx < H_C, q[:, :half], q[:, half:])
        qx_scr[...] = jnp.where(rx % H_C == cx, folded, 0.0).astype(BF16)
        m_scr[...] = jnp.full_like(m_scr, NEG_BIG)
        l_scr[...] = jnp.zeros_like(l_scr)
        acc_scr[...] = jnp.zeros_like(acc_scr)

    def softmax_step(s):
        m_old = m_scr[...]
        m_new = jnp.maximum(m_old, jnp.max(s, axis=1, keepdims=True))
        corr = jnp.exp(m_old - m_new)
        p = jnp.exp(s - m_new)
        l_scr[...] = l_scr[...] * corr + jnp.sum(p, axis=1, keepdims=True)
        m_scr[...] = m_new
        return p, corr

    kps = jnp.concatenate(
        [jnp.concatenate([ref[pl.ds(h8, npseudo, stride=H_C), :].astype(BF16) for h8 in range(H_C)], axis=1)
         for ref in kc_refs], axis=0)
    s = lax.dot_general(qx_scr[...], kps, (((1,), (1,)), ((), ())), preferred_element_type=F32)
    ncol = s.shape[1]
    col_par = lax.broadcasted_iota(jnp.int32, (nrow, ncol), 1) % 2
    row_par = lax.broadcasted_iota(jnp.int32, (nrow, ncol), 0) // (H_C * steps)
    p, corr = softmax_step(jnp.where(col_par == row_par, s, NEG_BIG))
    for h in range(H_C):
        rows = slice(h * grp, (h + 1) * grp)
        par = (2 * h) // H_C
        ph = p[rows, :]
        other = pltpu.roll(ph, 1 if par == 0 else ncol - 1, 1)
        vps = jnp.concatenate([ref[pl.ds(h, npseudo, stride=H_C), :].astype(BF16) for ref in vc_refs], axis=0)
        pv = jnp.dot(jnp.concatenate([ph, other], axis=0).astype(BF16), vps, preferred_element_type=F32)
        mine, theirs = pv[:grp], pv[grp:]
        both = jnp.concatenate([mine, theirs] if par == 0 else [theirs, mine], axis=1)
        acc_scr[rows, :] = acc_scr[rows, :] * corr[rows, :] + both

    @pl.when(pj == pl.num_programs(1) - 1)
    def _():
        pad = jnp.zeros((PAGE_SIZE - steps, kn_ref.shape[1]), F32)
        kn = jnp.concatenate([kn_ref[...], pad], axis=0).astype(BF16)
        vn = jnp.concatenate([vn_ref[...], pad], axis=0).astype(BF16)
        sn = lax.dot_general(qrow_scr[...], kn, (((1,), (1,)), ((), ())), preferred_element_type=F32)
        step = lax.broadcasted_iota(jnp.int32, (nrow, PAGE_SIZE), 0) % steps
        key = lax.broadcasted_iota(jnp.int32, (nrow, PAGE_SIZE), 1)
        pn, cn = softmax_step(jnp.where(key <= step, sn, NEG_BIG))
        pn = pn.astype(BF16)
        lam = _lambda(lam_ref, lam_init)
        linv = 1.0 / l_scr[...]
        for h in range(H_C):
            rows = slice(h * grp, (h + 1) * grp)
            o = (acc_scr[rows, :] * cn[rows, :] + jnp.dot(pn[rows, :], vn[:, h * DV_C:(h + 1) * DV_C],
                                                          preferred_element_type=F32)) * linv[rows, :]
            o_ref[:, h * DV_C:(h + 1) * DV_C] = _diff_combine(o[:steps], o[steps:], lam, subw_ref[...], lam_init)


def _decode_attn(q, kn, vn, cache_k, cache_v, page_table, lam4, subw, lam_init, steps):
    bsz, n_pages = page_table.shape
    nrow = 2 * H_C * steps
    tile = lambda n: pl.BlockSpec((steps, n), lambda b, p, pt: (b, 0))

    def page(shape, u):
        return pl.BlockSpec((None,) + shape,
                            lambda b, p, pt: (pt[b * n_pages + p * DECODE_PAGES + u],) + (0,) * len(shape))

    grid_spec = pltpu.PrefetchScalarGridSpec(
        num_scalar_prefetch=1,
        grid=(bsz, n_pages // DECODE_PAGES),
        in_specs=([tile(D_QK_C), tile(D_QK_C), tile(D_V_C)]
                  + [page((PAGE_SIZE * 2 * H_C, HD_C), u) for u in range(DECODE_PAGES)]
                  + [page((PAGE_SIZE * H_C * DV_C // LANES, LANES), u) for u in range(DECODE_PAGES)]
                  + [pl.BlockSpec((4, HD_C), lambda b, p, pt: (0, 0)),
                     pl.BlockSpec((1, DV_C), lambda b, p, pt: (0, 0))]),
        out_specs=tile(D_V_C),
        scratch_shapes=[pltpu.VMEM((nrow, H_C * HD_C), BF16), pltpu.VMEM((nrow, D_QK_C), BF16),
                        pltpu.VMEM((nrow, 1), F32), pltpu.VMEM((nrow, 1), F32), pltpu.VMEM((nrow, DV_C), F32)],
    )
    return pl.pallas_call(
        functools.partial(_decode_kernel, steps=steps, lam_init=lam_init),
        grid_spec=grid_spec,
        out_shape=jax.ShapeDtypeStruct((bsz * steps, D_V_C), F32),
        compiler_params=_params(("parallel", "arbitrary")),
        name="diff_attn_decode",
    )(page_table.reshape(-1), q, kn, vn, *([cache_k] * DECODE_PAGES), *([cache_v] * DECODE_PAGES), lam4, subw)


def _ffn_up_short_kernel(h_ref, wg_ref, wv_ref, cwg_ref, cwv_ref, cbg_ref, cbv_ref, bg_ref, bv_ref,
                         o_ref, ng_ref, nv_ref, *, nb, tt):
    c = wg_ref.shape[-1]
    h = h_ref[...]

    def branch(w_ref, cw_ref, cb_ref, buf_ref, new_ref):
        u = jnp.dot(h, w_ref[...], preferred_element_type=F32).reshape(nb, tt, c)
        y = _conv3(u, buf_ref[...], cw_ref[...]) + cb_ref[...]
        new_ref[...] = u[:, tt - 2:tt, :]
        return y.reshape(nb * tt, c)

    gate = branch(wg_ref, cwg_ref, cbg_ref, bg_ref, ng_ref)
    val = branch(wv_ref, cwv_ref, cbv_ref, bv_ref, nv_ref)
    o_ref[...] = (jax.nn.silu(gate) * val).astype(o_ref.dtype)


def _ffn_up_long_kernel(h_ref, wg_ref, wv_ref, cwg_ref, cwv_ref, cbg_ref, cbv_ref, bg_ref, bv_ref,
                        o_ref, ng_ref, nv_ref, cg_scr, cv_scr, *, tm):
    c = wg_ref.shape[-1]

    @pl.when(pl.program_id(1) == 0)
    def _():
        cg_scr[...] = bg_ref[...]
        cv_scr[...] = bv_ref[...]

    h = h_ref[...]

    def branch(w_ref, cw_ref, cb_ref, new_ref, carry_ref):
        u = jnp.dot(h, w_ref[...], preferred_element_type=F32).reshape(1, tm, c)
        y = _conv3(u, carry_ref[...], cw_ref[...]) + cb_ref[...]
        tail = u[:, tm - 2:tm, :]
        new_ref[...] = tail
        carry_ref[...] = tail
        return y.reshape(tm, c)

    gate = branch(wg_ref, cwg_ref, cbg_ref, ng_ref, cg_scr)
    val = branch(wv_ref, cwv_ref, cbv_ref, nv_ref, cv_scr)
    o_ref[...] = (jax.nn.silu(gate) * val).astype(o_ref.dtype)


def _ffn_up(h, w_up, layer, conv_w, conv_b, buf, seq_len, tm, tn):
    m, d = h.shape
    nj = D_FF // tn
    nb, tt = _seq_tiles(tm, seq_len)
    carried = nb == 1
    bsz = buf.shape[0]

    def cols(off):
        return lambda j, i: (0, j + off)

    def wcols(off):
        return lambda j, i: (layer, 0, j + off)

    def buf_spec(off):
        if carried:
            return pl.BlockSpec((1, 2, tn), lambda j, i: (0, 0, j + off))
        return pl.BlockSpec((nb, 2, tn), lambda j, i: (i, 0, j + off))

    if carried:
        body = functools.partial(_ffn_up_long_kernel, tm=tm)
        scratch = [pltpu.VMEM((1, 2, tn), F32), pltpu.VMEM((1, 2, tn), F32)]
    else:
        body = functools.partial(_ffn_up_short_kernel, nb=nb, tt=tt)
        scratch = []
    return pl.pallas_call(
        body,
        grid=(nj, m // tm),
        in_specs=[pl.BlockSpec((tm, d), lambda j, i: (i, 0)),
                  pl.BlockSpec((None, d, tn), wcols(0)), pl.BlockSpec((None, d, tn), wcols(nj)),
                  pl.BlockSpec((CONV_W, tn), cols(0)), pl.BlockSpec((CONV_W, tn), cols(nj)),
                  pl.BlockSpec((1, tn), cols(0)), pl.BlockSpec((1, tn), cols(nj)),
                  buf_spec(0), buf_spec(nj)],
        out_specs=[pl.BlockSpec((tm, tn), lambda j, i: (i, j)), buf_spec(0), buf_spec(0)],
        out_shape=[jax.ShapeDtypeStruct((m, D_FF), BF16),
                   jax.ShapeDtypeStruct((bsz, 2, D_FF), F32), jax.ShapeDtypeStruct((bsz, 2, D_FF), F32)],
        scratch_shapes=scratch,
        compiler_params=_params(("parallel", "arbitrary")),
        name="ffn_up",
    )(h, w_up, w_up, conv_w, conv_w, conv_b, conv_b, buf, buf)


def _pad_cols(x, n):
    return jnp.pad(x, [(0, 0)] * (x.ndim - 1) + [(0, n - x.shape[-1])])


def _trunk(x, conv_a, shift_b, wkv_b, ffn_buf, cache_k, cache_v, page_table, P, seq_len):
    m = x.shape[0]
    bsz = m // seq_len
    tm = 1024
    tm_small = 256

    h = _rms(x, P["norm_mix"][0], BF16)
    pa = _mm([h], [P["w_in_a"]], [(0, 0)], P_A, tm=tm, tn=1024, name="proj_in_a")
    pb = _mm([h], [P["w_in_b"]], [(0, 0)], P_B_PAD, tm=tm, tn=P_B_PAD // 3, name="proj_in_b")
    y_a, new_conv = _mixer_a(pa, P["conv_w_a"], conv_a, seq_len, 512)
    (r, lw, k2, v, kk, b, g, new_shift) = _prep_b(
        pb, shift_b, P["mu_b"], P["w0_b"], P["a0_b"], P["k_k_b"], P["k_a_b"],
        P["w2_b"], P["a2_b"], P["g2_b"], seq_len, tm_small)
    if seq_len % WKV_CHUNK == 0:
        y, new_wkv = _wkv_chunked(r, lw, k2, v, kk, b, wkv_b, seq_len, WKV_CHUNK)
    else:
        y, new_wkv = _wkv_chunked(r, lw, k2, v, kk, b, wkv_b, seq_len, seq_len, nseq=WKV_SHORT_NSEQ)
    y_b = _post_b(y, r, k2, v, g, P["ln_gain_b"], P["ln_bias_b"], P["r_k_b"], tm_small)
    x = _mm([y_a, y_b], [P["w_out_ab"], P["w_out_ab"]], [(0, 0), (1, 0)], D_MODEL,
            tm=tm, tn=1024, res=x, name="proj_out_ab")

    h = _rms(x, P["norm_ffn"][0], BF16)
    gated, nfg0, nfv0 = _ffn_up(h, P["w_up_f"], 0, P["conv_w_f"][0], P["conv_b_f"][0], ffn_buf[0],
                                seq_len, tm, 512)
    x = _mm([gated], [P["w_down_f"][0]], [(0, 0)], D_MODEL, tm=tm, tn=512, res=x, name="ffn_down")

    lam_init = 0.8 - 0.6 * math.exp(-0.3 * 1)
    h = _rms(x, P["norm_mix"][1], BF16)
    nq = D_QK_C // 1024
    if cache_k is None:
        qh = _mm([h], [P["w_qkv_c"]], [(0, 0)], D_QK_C, tm=tm, tn=1024, out_dtypes=(BF16,),
                 scale=ATTN_SCALE * math.log2(math.e), name="proj_q")
        k_rows, k_bf = _mm([h], [P["w_qkv_c"]], [(0, nq)], D_QK_C, tm=tm, tn=1024,
                           out_dtypes=(F32, BF16), name="proj_k")
        v_rows, v_bf = _mm([h], [P["w_qkv_c"]], [(0, 2 * nq)], D_V_C, tm=tm, tn=1024,
                           out_dtypes=(F32, BF16), name="proj_v")
        attn = _flash_prompt(qh, k_bf, v_bf, P["lam4"], P["subln_w_c"], lam_init, 1024, 512)
    else:
        qh = _mm([h], [P["w_qkv_c"]], [(0, 0)], D_QK_C, tm=tm, tn=1024, scale=ATTN_SCALE, name="proj_q")
        k_rows = _mm([h], [P["w_qkv_c"]], [(0, nq)], D_QK_C, tm=tm, tn=1024, name="proj_k")
        v_rows = _mm([h], [P["w_qkv_c"]], [(0, 2 * nq)], D_V_C, tm=tm, tn=1024, name="proj_v")
        attn = _decode_attn(qh, k_rows, v_rows, cache_k, cache_v, page_table, P["lam4"], P["subln_w_c"],
                            lam_init, seq_len)
    x = _mm([attn], [P["w_out_c"]], [(0, 0)], D_MODEL, tm=tm, tn=1024, res=x, name="proj_out_c")

    h = _rms(x, P["norm_ffn"][1], BF16)
    gated, nfg1, nfv1 = _ffn_up(h, P["w_up_f"], 1, P["conv_w_f"][1], P["conv_b_f"][1], ffn_buf[1],
                                seq_len, tm, 512)
    x = _mm([gated], [P["w_down_f"][1]], [(0, 0)], D_MODEL, tm=tm, tn=512, res=x, name="ffn_down")

    y_out = _rms(x, P["norm_final"], F32)
    new_ffn = jnp.stack([jnp.concatenate([nfg0, nfv0], axis=-1), jnp.concatenate([nfg1, nfv1], axis=-1)])
    return (y_out.reshape(bsz, seq_len, D_MODEL),
            new_conv[None],
            new_shift[None, :, 0, :P_B],
            new_wkv[None],
            k_rows.reshape(1, bsz, seq_len, 2 * H_C, HD_C),
            v_rows.reshape(1, bsz, seq_len, H_C, DV_C),
            new_ffn)


def kernel(x_prompt, x_sample, state_conv_a, state_shift_b, state_wkv_b, cache_k, cache_v, page_table,
           state_ffn_conv, norm_mix, norm_ffn, norm_final, w_in_ab, conv_w_a, mu_b, w0_b, w2_b, a0_b, a2_b,
           g2_b, k_k_b, k_a_b, r_k_b, ln_gain_b, ln_bias_b, w_out_ab, w_qkv_c, lambda_q1_c, lambda_k1_c,
           lambda_q2_c, lambda_k2_c, subln_w_c, w_out_c, w_up_f, conv_w_f, conv_b_f, w_down_f):
    bp, seq, d = x_prompt.shape
    bs, dec_seq, _ = x_sample.shape
    row = lambda a: a.reshape(1, -1)
    lora_rows = jnp.zeros((LORA_PAD, D_B), F32)
    P = dict(
        norm_mix=norm_mix, norm_ffn=norm_ffn, norm_final=norm_final,
        w_in_a=w_in_ab[0, :, :P_A].astype(BF16),
        w_in_b=_pad_cols(w_in_ab[0, :, P_A:], P_B_PAD).astype(BF16),
        conv_w_a=conv_w_a[0],
        mu_b=_pad_cols(row(mu_b[0]), P_B_PAD),
        w0_b=row(w0_b[0]), a0_b=row(a0_b[0]), k_k_b=row(k_k_b[0]), k_a_b=row(k_a_b[0]),
        w2_b=lora_rows.at[:R_W].set(w2_b[0]).astype(BF16),
        a2_b=lora_rows.at[R_W:R_W + R_A].set(a2_b[0]).astype(BF16),
        g2_b=lora_rows.at[R_W + R_A:R_W + R_A + R_G].set(g2_b[0]).astype(BF16),
        r_k_b=row(r_k_b[0]), ln_gain_b=row(ln_gain_b[0]), ln_bias_b=row(ln_bias_b[0]),
        w_out_ab=w_out_ab[0].astype(BF16),
        w_qkv_c=w_qkv_c[0].astype(BF16),
        lam4=jnp.stack([lambda_q1_c[0], lambda_k1_c[0], lambda_q2_c[0], lambda_k2_c[0]]),
        subln_w_c=row(subln_w_c[0]),
        w_out_c=w_out_c[0].astype(BF16),
        w_up_f=w_up_f.astype(BF16), conv_w_f=conv_w_f, conv_b_f=conv_b_f[:, None, :],
        w_down_f=w_down_f.astype(BF16),
    )
    n_phys = cache_k.shape[1]
    outs_p = _trunk(
        x_prompt.reshape(bp * seq, d),
        jnp.zeros((bp, CONV_W - 1, D_A), F32),
        jnp.zeros((bp, 1, P_B_PAD), F32),
        jnp.zeros((bp, H_B, HD_B, HD_B), F32),
        jnp.zeros((DEPTH, bp, CONV_W - 1, 2 * D_FF), F32),
        None, None, None, P, seq)
    outs_s = _trunk(
        x_sample.reshape(bs * dec_seq, d),
        state_conv_a[0],
        _pad_cols(state_shift_b[0], P_B_PAD)[:, None, :],
        state_wkv_b[0],
        state_ffn_conv,
        cache_k.reshape(n_phys, PAGE_SIZE * 2 * H_C, HD_C),
        cache_v.reshape(n_phys, PAGE_SIZE, H_C, DV_C // LANES, LANES).swapaxes(2, 3).reshape(
            n_phys, PAGE_SIZE * H_C * DV_C // LANES, LANES),
        page_table, P, dec_seq)
    return (outs_p[0], outs_s[0]) + outs_p[1:] + outs_s[1:]
```

```python
import functools
import math

import jax
import jax.numpy as jnp
from jax import lax
from jax.experimental import pallas as pl
from jax.experimental.pallas import tpu as pltpu

F32 = jnp.float32
BF16 = jnp.bfloat16

D_MODEL = 2048
DEPTH = 2
PAGE_SIZE = 128
CONV_W = 3
NORM_EPS = 1e-6
D_A = D_MODEL // 2
P_A = 3 * D_A
D_B = D_MODEL // 2
HD_B = 64
H_B = D_B // HD_B
R_W = 64
R_A = 64
R_G = 160
P_B = 3 * D_B + R_W + R_A + R_G
LORA_PAD = 384
P_B_PAD = 3 * D_B + LORA_PAD
GN_EPS_B = 64e-5
HD_C = 128
H_C = D_MODEL // (2 * HD_C)
DV_C = 2 * HD_C
D_QK_C = 2 * H_C * HD_C
D_V_C = H_C * DV_C
ATTN_SCALE = HD_C ** -0.5
SUBLN_EPS = 1e-5
D_FF = 5632

LANES = 128
SUBLANES = 8
VMEM_LIMIT = 52 * 1024 * 1024
NEG_BIG = -1e30


def _params(sem):
    return pltpu.CompilerParams(dimension_semantics=sem, vmem_limit_bytes=VMEM_LIMIT)


def _rms_kernel(x_ref, g_ref, o_ref):
    x = x_ref[...]
    y = x * lax.rsqrt(jnp.mean(x * x, axis=-1, keepdims=True) + NORM_EPS)
    o_ref[...] = (y * g_ref[...]).astype(o_ref.dtype)


def _rms(x, g, out_dtype, tm=512):
    m, d = x.shape
    return pl.pallas_call(
        _rms_kernel,
        grid=(m // tm,),
        in_specs=[pl.BlockSpec((tm, d), lambda i: (i, 0)), pl.BlockSpec((1, d), lambda i: (0, 0))],
        out_specs=pl.BlockSpec((tm, d), lambda i: (i, 0)),
        out_shape=jax.ShapeDtypeStruct((m, d), out_dtype),
        compiler_params=_params(("parallel",)),
        name="rmsnorm",
    )(x, g.reshape(1, d))


def _mm_kernel(*refs, nx, has_res, scale):
    xs, ws = refs[:nx], refs[nx:2 * nx]
    res = refs[2 * nx] if has_res else None
    outs = refs[2 * nx + int(has_res):]
    acc = None
    for x, w in zip(xs, ws):
        d = jnp.dot(x[...].astype(BF16), w[...], preferred_element_type=F32)
        acc = d if acc is None else acc + d
    if scale is not None:
        acc = acc * scale
    if res is not None:
        acc = acc + res[...]
    for o in outs:
        o[...] = acc.astype(o.dtype)


def _mm(xs, ws, wblocks, n, *, tm, tn, out_dtypes=(F32,), res=None, scale=None, name="matmul"):
    m = xs[0].shape[0]
    in_specs, args = [], []
    for x in xs:
        in_specs.append(pl.BlockSpec((tm, x.shape[1]), lambda i, j: (i, 0)))
        args.append(x)
    for x, w, (rb, cb) in zip(xs, ws, wblocks):
        in_specs.append(pl.BlockSpec((x.shape[1], tn), lambda i, j, rb=rb, cb=cb: (rb, cb + j)))
        args.append(w)
    if res is not None:
        in_specs.append(pl.BlockSpec((tm, tn), lambda i, j: (i, j)))
        args.append(res)
    outs = pl.pallas_call(
        functools.partial(_mm_kernel, nx=len(xs), has_res=res is not None, scale=scale),
        grid=(m // tm, n // tn),
        in_specs=in_specs,
        out_specs=[pl.BlockSpec((tm, tn), lambda i, j: (i, j)) for _ in out_dtypes],
        out_shape=[jax.ShapeDtypeStruct((m, n), dt) for dt in out_dtypes],
        compiler_params=_params(("parallel", "arbitrary")),
        name=name,
    )(*args)
    return outs if len(out_dtypes) > 1 else outs[0]


def _seq_tiles(tm, seq_len):
    return (tm // seq_len, seq_len) if seq_len < tm else (1, tm)


def _shift_rows(u3, prev, k):
    out = pltpu.roll(u3, k, 1)
    head = out[:, :SUBLANES, :]
    t = lax.broadcasted_iota(jnp.int32, head.shape, 1)
    for s in range(k):
        head = jnp.where(t == s, prev[:, 2 - k + s:3 - k + s, :], head)
    if u3.shape[1] == SUBLANES:
        return head
    return jnp.concatenate([head, out[:, SUBLANES:, :]], axis=1)


def _conv3(u3, prev, w):
    return w[0:1, :] * _shift_rows(u3, prev, 2) + w[1:2, :] * _shift_rows(u3, prev, 1) + w[2:3, :] * u3


def _mixer_a_kernel(gb_ref, gc_ref, xt_ref, w_ref, buf_ref, ya_ref, nc_ref, carry_ref, *, nb, tt, carried):
    c = gb_ref.shape[-1]
    cx = (gc_ref[...] * xt_ref[...]).reshape(nb, tt, c)
    if carried:
        @pl.when(pl.program_id(0) == 0)
        def _():
            carry_ref[...] = buf_ref[...]
        prev = carry_ref[...]
    else:
        prev = buf_ref[...]
    conv = _conv3(cx, prev, w_ref[...])
    ya_ref[...] = (gb_ref[...] * conv.reshape(nb * tt, c)).astype(ya_ref.dtype)
    tail = cx[:, tt - 2:tt, :]
    nc_ref[...] = tail
    if carried:
        carry_ref[...] = tail


def _mixer_a(pa, conv_w, buf, seq_len, tm):
    m = pa.shape[0]
    nb, tt = _seq_tiles(tm, seq_len)
    carried = nb == 1
    bsz = buf.shape[0]
    buf_spec = (pl.BlockSpec((1, 2, D_A), lambda i: (0, 0, 0)) if carried
                else pl.BlockSpec((nb, 2, D_A), lambda i: (i, 0, 0)))
    ya, nc = pl.pallas_call(
        functools.partial(_mixer_a_kernel, nb=nb, tt=tt, carried=carried),
        grid=(m // tm,),
        in_specs=[pl.BlockSpec((tm, D_A), lambda i: (i, 0)),
                  pl.BlockSpec((tm, D_A), lambda i: (i, 1)),
                  pl.BlockSpec((tm, D_A), lambda i: (i, 2)),
                  pl.BlockSpec((CONV_W, D_A), lambda i: (0, 0)),
                  buf_spec],
        out_specs=[pl.BlockSpec((tm, D_A), lambda i: (i, 0)), buf_spec],
        out_shape=[jax.ShapeDtypeStruct((m, D_A), BF16), jax.ShapeDtypeStruct((bsz, 2, D_A), F32)],
        scratch_shapes=[pltpu.VMEM((1, 2, D_A), F32)],
        compiler_params=_params(("arbitrary",)),
        name="mixer_a",
    )(pa, pa, pa, conv_w, buf)
    return ya, nc


def _split_bf16(x):
    hi = x.astype(BF16)
    lo = (x - hi.astype(F32)).astype(BF16)
    return hi, lo


def _head_sum(x):
    blk = 2 * LANES
    r = lax.broadcasted_iota(jnp.int32, (blk, blk), 0) // HD_B
    c = lax.broadcasted_iota(jnp.int32, (blk, blk), 1) // HD_B
    ones = jnp.where(r == c, 1.0, 0.0).astype(BF16)
    hi, lo = _split_bf16(x)
    parts = []
    for s in range(x.shape[1] // blk):
        sl = slice(s * blk, (s + 1) * blk)
        parts.append(jnp.dot(hi[:, sl], ones, preferred_element_type=F32)
                     + jnp.dot(lo[:, sl], ones, preferred_element_type=F32))
    return jnp.concatenate(parts, axis=1)


def _prep_b_kernel(pb_ref, prev_ref, mu_ref, w0_ref, a0_ref, kk_ref, ka_ref, w2_ref, a2_ref, g2_ref,
                   r_out, lw_out, k_out, v_out, kk_out, b_out, g_out, ns_out,
                   carry_ref, *, nb, tt, carried):
    c = pb_ref.shape[-1]
    pb3 = pb_ref[...].reshape(nb, tt, c)
    if carried:
        @pl.when(pl.program_id(0) == 0)
        def _():
            carry_ref[...] = prev_ref[...]
        prev = carry_ref[...]
    else:
        prev = prev_ref[...]
    t = lax.broadcasted_iota(jnp.int32, pb3.shape, 1)
    shifted = jnp.where(t == 0, prev, pltpu.roll(pb3, 1, 1))
    last = pb3[:, tt - 1:tt, :]
    ns_out[...] = last
    if carried:
        carry_ref[...] = last
    x = (pb3 + mu_ref[...] * (shifted - pb3)).reshape(nb * tt, c)
    r = x[:, :D_B]
    k = x[:, D_B:2 * D_B]
    v = x[:, 2 * D_B:3 * D_B]
    lo = x[:, 3 * D_B:]
    wl = w0_ref[...] + jnp.dot(jnp.tanh(lo).astype(BF16), w2_ref[...], preferred_element_type=F32)
    softplus_neg = jnp.maximum(-wl, 0.0) + jnp.log(1.0 + jnp.exp(-jnp.abs(wl)))
    w_log = -softplus_neg - 0.5
    a = jax.nn.sigmoid(a0_ref[...] + jnp.dot(lo.astype(BF16), a2_ref[...], preferred_element_type=F32))
    g = jnp.dot(jax.nn.sigmoid(lo).astype(BF16), g2_ref[...], preferred_element_type=F32)
    kk = k * kk_ref[...]
    kk = kk / jnp.maximum(jnp.sqrt(_head_sum(kk * kk)), 1e-12)
    k2 = k * (1.0 + (a - 1.0) * ka_ref[...])
    b = kk * a
    r_out[...] = r
    lw_out[...] = -jnp.exp(w_log)
    k_out[...] = k2
    v_out[...] = v
    kk_out[...] = kk
    b_out[...] = b
    g_out[...] = g


def _prep_b(pb, shift_prev, mu, w0, a0, k_k, k_a, w2p, a2p, g2p, seq_len, tm):
    m = pb.shape[0]
    nb, tt = _seq_tiles(tm, seq_len)
    carried = nb == 1
    bsz = shift_prev.shape[0]
    prev_spec = (pl.BlockSpec((1, 1, P_B_PAD), lambda i: (0, 0, 0)) if carried
                 else pl.BlockSpec((nb, 1, P_B_PAD), lambda i: (i, 0, 0)))
    row = lambda n: pl.BlockSpec((1, n), lambda i: (0, 0))
    lora = pl.BlockSpec((LORA_PAD, D_B), lambda i: (0, 0))
    tile = pl.BlockSpec((tm, D_B), lambda i: (i, 0))
    outs = pl.pallas_call(
        functools.partial(_prep_b_kernel, nb=nb, tt=tt, carried=carried),
        grid=(m // tm,),
        in_specs=[pl.BlockSpec((tm, P_B_PAD), lambda i: (i, 0)), prev_spec, row(P_B_PAD),
                  row(D_B), row(D_B), row(D_B), row(D_B), lora, lora, lora],
        out_specs=[tile] * 7 + [prev_spec],
        out_shape=[jax.ShapeDtypeStruct((m, D_B), F32)] * 7 + [jax.ShapeDtypeStruct((bsz, 1, P_B_PAD), F32)],
        scratch_shapes=[pltpu.VMEM((1, 1, P_B_PAD), F32)],
        compiler_params=_params(("arbitrary",)),
        name="rwkv_prep",
    )(pb, shift_prev, mu, w0, a0, k_k, k_a, w2p, a2p, g2p)
    return outs


def _split3_bf16(x):
    hi = x.astype(BF16)
    rest = x - hi.astype(F32)
    mid = rest.astype(BF16)
    return hi, mid, (rest - mid.astype(F32)).astype(BF16)


def _dot_bf16(a, b, dims=(((1,), (0,)), ((), ()))):
    return lax.dot_general(a.astype(BF16), b.astype(BF16), dims, preferred_element_type=F32)


_NT = (((1,), (1,)), ((), ()))
_TN = (((0,), (0,)), ((), ()))
WKV_CHUNK = 128
WKV_SHORT_NSEQ = 8


def _wkv_chunk_kernel(r_ref, lw_ref, k_ref, v_ref, kk_ref, b_ref, s0_ref, y_ref, sf_ref, s_scr, *, c, nseq):
    ci = pl.program_id(1)
    npair = H_B // 2
    rows = nseq * c

    @pl.when(ci == 0)
    def _():
        s_scr[...] = jnp.zeros_like(s_scr)
        for q in range(nseq):
            for j in range(npair):
                s_scr[q * npair + j, 0:HD_B, 0:HD_B] = s0_ref[q, 2 * j]
                s_scr[q * npair + j, HD_B:, HD_B:] = s0_ref[q, 2 * j + 1]

    ti = lax.broadcasted_iota(jnp.int32, (rows, rows), 0)
    tj = lax.broadcasted_iota(jnp.int32, (rows, rows), 1)
    tri = jnp.where(jnp.logical_and(ti // c == tj // c, tj <= ti), 1.0, 0.0).astype(BF16)
    lw = lw_ref[...]
    cum = None
    for limb in _split3_bf16(lw):
        d = jnp.dot(tri, limb, preferred_element_type=F32)
        cum = d if cum is None else cum + d
    cum3 = cum.reshape(nseq, c, D_B)
    last3 = cum3[:, c - 1:c, :]
    mid3 = cum3[:, c // 2 - 1:c // 2, :]
    cc = (cum3 - mid3).reshape(rows, D_B)
    g_mid3 = jnp.exp(mid3)
    g_last3 = jnp.exp(last3)
    g_inv = jnp.exp(-cc)
    kt = kk_ref[...] * jnp.exp(cc - lw)
    rt = r_ref[...] * jnp.exp(cc)
    bt = b_ref[...] * g_inv
    kq = k_ref[...] * g_inv
    to_end = jnp.exp(last3 - cum3).reshape(rows, D_B)
    kh = k_ref[...] * to_end
    bh = b_ref[...] * to_end
    v = v_ref[...]

    ci_ = lax.broadcasted_iota(jnp.int32, (c, c), 0)
    cj_ = lax.broadcasted_iota(jnp.int32, (c, c), 1)
    first = lax.broadcasted_iota(jnp.int32, (c, LANES), 1) < HD_B
    col2 = lax.broadcasted_iota(jnp.int32, (c, 2 * c), 1)
    row2 = lax.broadcasted_iota(jnp.int32, (c, 2 * c), 0)
    strict = (col2 % c) < row2
    incl = (col2 % c) <= row2
    second_grp = col2 >= c
    strict_sq = cj_ < ci_
    blockdiag = (lax.broadcasted_iota(jnp.int32, (LANES, LANES), 0) // HD_B
                 == lax.broadcasted_iota(jnp.int32, (LANES, LANES), 1) // HD_B)
    lane_l = lax.broadcasted_iota(jnp.int32, (2 * c, LANES), 1)
    nsq = (c - 1).bit_length() - 1

    units = [(q, j) for q in range(nseq) for j in range(npair)]
    rsl = [slice(q * c, (q + 1) * c) for q, j in units]
    sls = [slice(j * LANES, (j + 1) * LANES) for q, j in units]
    s2s = [s_scr[q * npair + j] for q, j in units]
    lhs2s = [jnp.concatenate([kt[rs, sl], rt[rs, sl]], axis=0) for rs, sl in zip(rsl, sls)]
    rhs2s = [jnp.concatenate([bt[rs, sl], kq[rs, sl]], axis=0) for rs, sl in zip(rsl, sls)]
    vps = [v[rs, sl] for rs, sl in zip(rsl, sls)]
    wy0s = [_dot_bf16(lhs2s[n], s2s[n] * g_mid3[q][:, sls[n]], _NT)
            for n, (q, j) in enumerate(units)]
    ns, rs_, xs = [], [], []
    for n in range(len(units)):
        for hh in range(2):
            mine = (lane_l < HD_B) if hh == 0 else (lane_l >= HD_B)
            m1 = _dot_bf16(jnp.where(mine, lhs2s[n], 0.0), rhs2s[n], _NT)
            ab_ak = jnp.where(strict, m1[:c], 0.0)
            rs_.append(jnp.where(incl, m1[c:], 0.0))
            ns.append(jnp.where(strict_sq, m1[:c, :c], 0.0))
            xs.append(wy0s[n][:c] + _dot_bf16(jnp.where(second_grp, ab_ak, 0.0),
                                              jnp.concatenate([vps[n], vps[n]], axis=0)))
    xs = [x - _dot_bf16(nm, x) for nm, x in zip(ns, xs)]
    ps = ns
    for _ in range(nsq):
        ps = [_dot_bf16(p, p) for p in ps]
        xs = [x + _dot_bf16(p, x) for p, x in zip(ps, xs)]
    for n, (q, j) in enumerate(units):
        u = jnp.where(first, xs[2 * n], xs[2 * n + 1])
        stack = jnp.concatenate([-u, vps[n]], axis=0)
        y_ref[rsl[n], sls[n]] = wy0s[n][c:] + jnp.where(first, _dot_bf16(rs_[2 * n], stack),
                                                        _dot_bf16(rs_[2 * n + 1], stack))
        ds = _dot_bf16(jnp.concatenate([vps[n], -u], axis=0),
                       jnp.concatenate([kh[rsl[n], sls[n]], bh[rsl[n], sls[n]]], axis=0), _TN)
        s_scr[q * npair + j] = s2s[n] * g_last3[q][:, sls[n]] + jnp.where(blockdiag, ds, 0.0)

    @pl.when(ci == pl.num_programs(1) - 1)
    def _():
        for q in range(nseq):
            for j in range(npair):
                s2 = s_scr[q * npair + j]
                sf_ref[q, 2 * j] = s2[0:HD_B, 0:HD_B]
                sf_ref[q, 2 * j + 1] = s2[HD_B:, HD_B:]


def _wkv_chunked(r, lw, k, v, kk, b, s0, seq_len, c, nseq=1):
    bsz = s0.shape[0]
    nc = seq_len // c
    assert nseq == 1 or nc == 1
    tile = pl.BlockSpec((nseq * c, D_B), lambda bi, ci: (bi * nc + ci, 0))
    st = pl.BlockSpec((nseq, H_B, HD_B, HD_B), lambda bi, ci: (bi, 0, 0, 0))
    return pl.pallas_call(
        functools.partial(_wkv_chunk_kernel, c=c, nseq=nseq),
        grid=(bsz // nseq, nc),
        in_specs=[tile] * 6 + [st],
        out_specs=[tile, st],
        out_shape=[jax.ShapeDtypeStruct((bsz * seq_len, D_B), F32),
                   jax.ShapeDtypeStruct((bsz, H_B, HD_B, HD_B), F32)],
        scratch_shapes=[pltpu.VMEM((nseq * H_B // 2, LANES, LANES), F32)],
        compiler_params=_params(("parallel", "arbitrary")),
        name="wkv_chunked",
    )(r, lw, k, v, kk, b, s0)


def _post_b_kernel(y_ref, r_ref, k_ref, v_ref, g_ref, gain_ref, bias_ref, rk_ref, o_ref):
    y = y_ref[...]
    d = y - _head_sum(y) * (1.0 / HD_B)
    var = _head_sum(d * d) * (1.0 / HD_B)
    yn = d * lax.rsqrt(var + GN_EPS_B) * gain_ref[...] + bias_ref[...]
    v = v_ref[...]
    bonus = _head_sum(r_ref[...] * k_ref[...] * rk_ref[...]) * v
    o_ref[...] = ((yn + bonus) * g_ref[...]).astype(o_ref.dtype)


def _post_b(y, r, k, v, g, gain, bias, r_k, tm):
    m = y.shape[0]
    tile = pl.BlockSpec((tm, D_B), lambda i: (i, 0))
    row = pl.BlockSpec((1, D_B), lambda i: (0, 0))
    return pl.pallas_call(
        _post_b_kernel,
        grid=(m // tm,),
        in_specs=[tile] * 5 + [row] * 3,
        out_specs=tile,
        out_shape=jax.ShapeDtypeStruct((m, D_B), BF16),
        compiler_params=_params(("parallel",)),
        name="rwkv_post",
    )(y, r, k, v, g, gain, bias, r_k)


def _lambda(lam_ref, lam_init):
    l4 = lam_ref[...]
    s1 = jnp.sum(l4[0:1, :] * l4[1:2, :], axis=1, keepdims=True)
    s2 = jnp.sum(l4[2:3, :] * l4[3:4, :], axis=1, keepdims=True)
    return jnp.exp(s1) - jnp.exp(s2) + lam_init


def _diff_combine(o0, o1, lam, subw, lam_init):
    o = o0 - lam * o1
    o = o * lax.rsqrt(jnp.mean(o * o, axis=-1, keepdims=True) + SUBLN_EPS) * subw
    return o * (1.0 - lam_init)


def _flash_kernel(q_ref, k_ref, v_ref, lam_ref, subw_ref, o_ref, m_scr, l_scr, acc_scr, *, tq, tk, lam_init):
    qi = pl.program_id(1)
    m_scr[...] = jnp.full_like(m_scr, NEG_BIG)
    l_scr[...] = jnp.zeros_like(l_scr)
    acc_scr[...] = jnp.zeros_like(acc_scr)

    def block(off, diag):
        kb = k_ref[pl.ds(off, tk), :]
        vb = v_ref[pl.ds(off, tk), :]
        r0 = 0 if diag is None else diag * tk
        rows = slice(r0, tq)
        if diag is not None:
            row = lax.broadcasted_iota(jnp.int32, (tq - r0, tk), 0)
            col = lax.broadcasted_iota(jnp.int32, (tq - r0, tk), 1)
            keep = col <= row
        for i in range(2):
            sl = slice(i * HD_C, (i + 1) * HD_C)
            s = lax.dot_general(q_ref[rows, sl], kb[:, sl], (((1,), (1,)), ((), ())),
                                preferred_element_type=F32)
            if diag is not None:
                s = jnp.where(keep, s, NEG_BIG)
            m_old = m_scr[i, rows, :]
            m_new = jnp.maximum(m_old, jnp.max(s, axis=1, keepdims=True))
            corr = jnp.exp2(m_old - m_new)
            p = jnp.exp2(s - jnp.concatenate([m_new] * (tk // LANES), axis=1))
            l_scr[i, rows, :] = l_scr[i, rows, :] * corr + jnp.sum(p, axis=1, keepdims=True)
            acc_scr[i, rows, :] = (acc_scr[i, rows, :] * jnp.concatenate([corr] * (DV_C // LANES), axis=1)
                                   + jnp.dot(p.astype(BF16), vb, preferred_element_type=F32))
            m_scr[i, rows, :] = m_new

    def full_block(j, carry):
        block(pl.multiple_of(j * tk, tk), None)
        return carry

    lax.fori_loop(0, (qi * tq) // tk, full_block, 0)
    for d in range(tq // tk):
        block(pl.multiple_of(qi * tq + d * tk, tk), d)

    lam = _lambda(lam_ref, lam_init)
    o0 = acc_scr[0] / jnp.concatenate([l_scr[0]] * (DV_C // LANES), axis=1)
    o1 = acc_scr[1] / jnp.concatenate([l_scr[1]] * (DV_C // LANES), axis=1)
    o_ref[...] = _diff_combine(o0, o1, lam, subw_ref[...], lam_init).astype(o_ref.dtype)


def _flash_prompt(q, k, v, lam4, subw, lam_init, tq, tk):
    s = q.shape[0]
    blk = 2 * HD_C
    return pl.pallas_call(
        functools.partial(_flash_kernel, tq=tq, tk=tk, lam_init=lam_init),
        grid=(H_C, s // tq),
        in_specs=[pl.BlockSpec((tq, blk), lambda h, i: (i, h)),
                  pl.BlockSpec((s, blk), lambda h, i: (0, h)),
                  pl.BlockSpec((s, DV_C), lambda h, i: (0, h)),
                  pl.BlockSpec((4, HD_C), lambda h, i: (0, 0)),
                  pl.BlockSpec((1, DV_C), lambda h, i: (0, 0))],
        out_specs=pl.BlockSpec((tq, DV_C), lambda h, i: (i, h)),
        out_shape=jax.ShapeDtypeStruct((s, D_V_C), BF16),
        scratch_shapes=[pltpu.VMEM((2, tq, LANES), F32), pltpu.VMEM((2, tq, LANES), F32),
                        pltpu.VMEM((2, tq, DV_C), F32)],
        compiler_params=_params(("parallel", "arbitrary")),
        name="diff_attn_prompt",
    )(q, k, v, lam4, subw)


DECODE_PAGES = 8


def _decode_kernel(pt_ref, q_ref, kn_ref, vn_ref, *rest, steps, lam_init):
    del pt_ref
    kc_refs, vc_refs = rest[:DECODE_PAGES], rest[DECODE_PAGES:2 * DECODE_PAGES]
    lam_ref, subw_ref, o_ref, qx_scr, qrow_scr, m_scr, l_scr, acc_scr = rest[2 * DECODE_PAGES:]
    pj = pl.program_id(1)
    nrow = 2 * H_C * steps
    grp = 2 * steps
    npseudo = 2 * PAGE_SIZE
    half = H_C * HD_C

    @pl.when(pj == 0)
    def _():
        q = jnp.tile(q_ref[...], (2 * H_C, 1))
        r = lax.broadcasted_iota(jnp.int32, q.shape, 0) // steps
        c = lax.broadcasted_iota(jnp.int32, q.shape, 1) // HD_C
        qrow_scr[...] = jnp.where(r == c, q, 0.0).astype(BF16)
        rx = lax.broadcasted_iota(jnp.int32, (nrow, half), 0) // steps
        cx = lax.broadcasted_iota(jnp.int32, (nrow, half), 1) // HD_C
        folded = jnp.where(rx < H_C, q[:, :half], q[:, half:])
        qx_scr[...] = jnp.where(rx % H_C == cx, folded, 0.0).astype(BF16)
        m_scr[...] = jnp.full_like(m_scr, NEG_BIG)
        l_scr[...] = jnp.zeros_like(l_scr)
        acc_scr[...] = jnp.zeros_like(acc_scr)

    def softmax_step(s):
        m_old = m_scr[...]
        m_new = jnp.maximum(m_old, jnp.max(s, axis=1, keepdims=True))
        corr = jnp.exp(m_old - m_new)
        p = jnp.exp(s - m_new)
        l_scr[...] = l_scr[...] * corr + jnp.sum(p, axis=1, keepdims=True)
        m_scr[...] = m_new
        return p, corr

    kps = jnp.concatenate(
        [jnp.concatenate([ref[pl.ds(h8, npseudo, stride=H_C), :].astype(BF16) for h8 in range(H_C)], axis=1)
         for ref in kc_refs], axis=0)
    s = lax.dot_general(qx_scr[...], kps, (((1,), (1,)), ((), ())), preferred_element_type=F32)
    ncol = s.shape[1]
    col_par = lax.broadcasted_iota(jnp.int32, (nrow, ncol), 1) % 2
    row_par = lax.broadcasted_iota(jnp.int32, (nrow, ncol), 0) // (H_C * steps)
    p, corr = softmax_step(jnp.where(col_par == row_par, s, NEG_BIG))
    for h in range(H_C):
        rows = slice(h * grp, (h + 1) * grp)
        par = (2 * h) // H_C
        ph = p[rows, :]
        other = pltpu.roll(ph, 1 if par == 0 else ncol - 1, 1)
        vps = jnp.concatenate([ref[pl.ds(h, npseudo, stride=H_C), :].astype(BF16) for ref in vc_refs], axis=0)
        pv = jnp.dot(jnp.concatenate([ph, other], axis=0).astype(BF16), vps, preferred_element_type=F32)
        mine, theirs = pv[:grp], pv[grp:]
        both = jnp.concatenate([mine, theirs] if par == 0 else [theirs, mine], axis=1)
        acc_scr[rows, :] = acc_scr[rows, :] * corr[rows, :] + both

    @pl.when(pj == pl.num_programs(1) - 1)
    def _():
        pad = jnp.zeros((PAGE_SIZE - steps, kn_ref.shape[1]), F32)
        kn = jnp.concatenate([kn_ref[...], pad], axis=0).astype(BF16)
        vn = jnp.concatenate([vn_ref[...], pad], axis=0).astype(BF16)
        sn = lax.dot_general(qrow_scr[...], kn, (((1,), (1,)), ((), ())), preferred_element_type=F32)
        step = lax.broadcasted_iota(jnp.int32, (nrow, PAGE_SIZE), 0) % steps
        key = lax.broadcasted_iota(jnp.int32, (nrow, PAGE_SIZE), 1)
        pn, cn = softmax_step(jnp.where(key <= step, sn, NEG_BIG))
        pn = pn.astype(BF16)
        lam = _lambda(lam_ref, lam_init)
        linv = 1.0 / l_scr[...]
        for h in range(H_C):
            rows = slice(h * grp, (h + 1) * grp)
            o = (acc_scr[rows, :] * cn[rows, :] + jnp.dot(pn[rows, :], vn[:, h * DV_C:(h + 1) * DV_C],
                                                          preferred_element_type=F32)) * linv[rows, :]
            o_ref[:, h * DV_C:(h + 1) * DV_C] = _diff_combine(o[:steps], o[steps:], lam, subw_ref[...], lam_init)


def _decode_attn(q, kn, vn, cache_k, cache_v, page_table, lam4, subw, lam_init, steps):
    bsz, n_pages = page_table.shape
    nrow = 2 * H_C * steps
    tile = lambda n: pl.BlockSpec((steps, n), lambda b, p, pt: (b, 0))

    def page(shape, u):
        return pl.BlockSpec((None,) + shape,
                            lambda b, p, pt: (pt[b * n_pages + p * DECODE_PAGES + u],) + (0,) * len(shape))

    grid_spec = pltpu.PrefetchScalarGridSpec(
        num_scalar_prefetch=1,
        grid=(bsz, n_pages // DECODE_PAGES),
        in_specs=([tile(D_QK_C), tile(D_QK_C), tile(D_V_C)]
                  + [page((PAGE_SIZE * 2 * H_C, HD_C), u) for u in range(DECODE_PAGES)]
                  + [page((PAGE_SIZE * H_C * DV_C // LANES, LANES), u) for u in range(DECODE_PAGES)]
                  + [pl.BlockSpec((4, HD_C), lambda b, p, pt: (0, 0)),
                     pl.BlockSpec((1, DV_C), lambda b, p, pt: (0, 0))]),
        out_specs=tile(D_V_C),
        scratch_shapes=[pltpu.VMEM((nrow, H_C * HD_C), BF16), pltpu.VMEM((nrow, D_QK_C), BF16),
                        pltpu.VMEM((nrow, 1), F32), pltpu.VMEM((nrow, 1), F32), pltpu.VMEM((nrow, DV_C), F32)],
    )
    return pl.pallas_call(
        functools.partial(_decode_kernel, steps=steps, lam_init=lam_init),
        grid_spec=grid_spec,
        out_shape=jax.ShapeDtypeStruct((bsz * steps, D_V_C), F32),
        compiler_params=_params(("parallel", "arbitrary")),
        name="diff_attn_decode",
    )(page_table.reshape(-1), q, kn, vn, *([cache_k] * DECODE_PAGES), *([cache_v] * DECODE_PAGES), lam4, subw)


def _ffn_up_short_kernel(h_ref, wg_ref, wv_ref, cwg_ref, cwv_ref, cbg_ref, cbv_ref, bg_ref, bv_ref,
                         o_ref, ng_ref, nv_ref, *, nb, tt):
    c = wg_ref.shape[-1]
    h = h_ref[...]

    def branch(w_ref, cw_ref, cb_ref, buf_ref, new_ref):
        u = jnp.dot(h, w_ref[...], preferred_element_type=F32).reshape(nb, tt, c)
        y = _conv3(u, buf_ref[...], cw_ref[...]) + cb_ref[...]
        new_ref[...] = u[:, tt - 2:tt, :]
        return y.reshape(nb * tt, c)

    gate = branch(wg_ref, cwg_ref, cbg_ref, bg_ref, ng_ref)
    val = branch(wv_ref, cwv_ref, cbv_ref, bv_ref, nv_ref)
    o_ref[...] = (jax.nn.silu(gate) * val).astype(o_ref.dtype)


def _ffn_up_long_kernel(h_ref, wg_ref, wv_ref, cwg_ref, cwv_ref, cbg_ref, cbv_ref, bg_ref, bv_ref,
                        o_ref, ng_ref, nv_ref, cg_scr, cv_scr, *, tm):
    c = wg_ref.shape[-1]

    @pl.when(pl.program_id(1) == 0)
    def _():
        cg_scr[...] = bg_ref[...]
        cv_scr[...] = bv_ref[...]

    h = h_ref[...]

    def branch(w_ref, cw_ref, cb_ref, new_ref, carry_ref):
        u = jnp.dot(h, w_ref[...], preferred_element_type=F32).reshape(1, tm, c)
        y = _conv3(u, carry_ref[...], cw_ref[...]) + cb_ref[...]
        tail = u[:, tm - 2:tm, :]
        new_ref[...] = tail
        carry_ref[...] = tail
        return y.reshape(tm, c)

    gate = branch(wg_ref, cwg_ref, cbg_ref, ng_ref, cg_scr)
    val = branch(wv_ref, cwv_ref, cbv_ref, nv_ref, cv_scr)
    o_ref[...] = (jax.nn.silu(gate) * val).astype(o_ref.dtype)


def _ffn_up(h, w_up, layer, conv_w, conv_b, buf, seq_len, tm, tn):
    m, d = h.shape
    nj = D_FF // tn
    nb, tt = _seq_tiles(tm, seq_len)
    carried = nb == 1
    bsz = buf.shape[0]

    def cols(off):
        return lambda j, i: (0, j + off)

    def wcols(off):
        return lambda j, i: (layer, 0, j + off)

    def buf_spec(off):
        if carried:
            return pl.BlockSpec((1, 2, tn), lambda j, i: (0, 0, j + off))
        return pl.BlockSpec((nb, 2, tn), lambda j, i: (i, 0, j + off))

    if carried:
        body = functools.partial(_ffn_up_long_kernel, tm=tm)
        scratch = [pltpu.VMEM((1, 2, tn), F32), pltpu.VMEM((1, 2, tn), F32)]
    else:
        body = functools.partial(_ffn_up_short_kernel, nb=nb, tt=tt)
        scratch = []
    return pl.pallas_call(
        body,
        grid=(nj, m // tm),
        in_specs=[pl.BlockSpec((tm, d), lambda j, i: (i, 0)),
                  pl.BlockSpec((None, d, tn), wcols(0)), pl.BlockSpec((None, d, tn), wcols(nj)),
                  pl.BlockSpec((CONV_W, tn), cols(0)), pl.BlockSpec((CONV_W, tn), cols(nj)),
                  pl.BlockSpec((1, tn), cols(0)), pl.BlockSpec((1, tn), cols(nj)),
                  buf_spec(0), buf_spec(nj)],
        out_specs=[pl.BlockSpec((tm, tn), lambda j, i: (i, j)), buf_spec(0), buf_spec(0)],
        out_shape=[jax.ShapeDtypeStruct((m, D_FF), BF16),
                   jax.ShapeDtypeStruct((bsz, 2, D_FF), F32), jax.ShapeDtypeStruct((bsz, 2, D_FF), F32)],
        scratch_shapes=scratch,
        compiler_params=_params(("parallel", "arbitrary")),
        name="ffn_up",
    )(h, w_up, w_up, conv_w, conv_w, conv_b, conv_b, buf, buf)


def _pad_cols(x, n):
    return jnp.pad(x, [(0, 0)] * (x.ndim - 1) + [(0, n - x.shape[-1])])


def _trunk(x, conv_a, shift_b, wkv_b, ffn_buf, cache_k, cache_v, page_table, P, seq_len):
    m = x.shape[0]
    bsz = m // seq_len
    tm = 1024
    tm_small = 256

    h = _rms(x, P["norm_mix"][0], BF16)
    pa = _mm([h], [P["w_in_a"]], [(0, 0)], P_A, tm=tm, tn=1024, name="proj_in_a")
    pb = _mm([h], [P["w_in_b"]], [(0, 0)], P_B_PAD, tm=tm, tn=P_B_PAD // 3, name="proj_in_b")
    y_a, new_conv = _mixer_a(pa, P["conv_w_a"], conv_a, seq_len, 512)
    (r, lw, k2, v, kk, b, g, new_shift) = _prep_b(
        pb, shift_b, P["mu_b"], P["w0_b"], P["a0_b"], P["k_k_b"], P["k_a_b"],
        P["w2_b"], P["a2_b"], P["g2_b"], seq_len, tm_small)
    if seq_len % WKV_CHUNK == 0:
        y, new_wkv = _wkv_chunked(r, lw, k2, v, kk, b, wkv_b, seq_len, WKV_CHUNK)
    else:
        y, new_wkv = _wkv_chunked(r, lw, k2, v, kk, b, wkv_b, seq_len, seq_len, nseq=WKV_SHORT_NSEQ)
    y_b = _post_b(y, r, k2, v, g, P["ln_gain_b"], P["ln_bias_b"], P["r_k_b"], tm_small)
    x = _mm([y_a, y_b], [P["w_out_ab"], P["w_out_ab"]], [(0, 0), (1, 0)], D_MODEL,
            tm=tm, tn=1024, res=x, name="proj_out_ab")

    h = _rms(x, P["norm_ffn"][0], BF16)
    gated, nfg0, nfv0 = _ffn_up(h, P["w_up_f"], 0, P["conv_w_f"][0], P["conv_b_f"][0], ffn_buf[0],
                                seq_len, tm, 512)
    x = _mm([gated], [P["w_down_f"][0]], [(0, 0)], D_MODEL, tm=tm, tn=512, res=x, name="ffn_down")

    lam_init = 0.8 - 0.6 * math.exp(-0.3 * 1)
    h = _rms(x, P["norm_mix"][1], BF16)
    nq = D_QK_C // 1024
    if cache_k is None:
        qh = _mm([h], [P["w_qkv_c"]], [(0, 0)], D_QK_C, tm=tm, tn=1024, out_dtypes=(BF16,),
                 scale=ATTN_SCALE * math.log2(math.e), name="proj_q")
        k_rows, k_bf = _mm([h], [P["w_qkv_c"]], [(0, nq)], D_QK_C, tm=tm, tn=1024,
                           out_dtypes=(F32, BF16), name="proj_k")
        v_rows, v_bf = _mm([h], [P["w_qkv_c"]], [(0, 2 * nq)], D_V_C, tm=tm, tn=1024,
                           out_dtypes=(F32, BF16), name="proj_v")
        attn = _flash_prompt(qh, k_bf, v_bf, P["lam4"], P["subln_w_c"], lam_init, 1024, 512)
    else:
        qh = _mm([h], [P["w_qkv_c"]], [(0, 0)], D_QK_C, tm=tm, tn=1024, scale=ATTN_SCALE, name="proj_q")
        k_rows = _mm([h], [P["w_qkv_c"]], [(0, nq)], D_QK_C, tm=tm, tn=1024, name="proj_k")
        v_rows = _mm([h], [P["w_qkv_c"]], [(0, 2 * nq)], D_V_C, tm=tm, tn=1024, name="proj_v")
        attn = _decode_attn(qh, k_rows, v_rows, cache_k, cache_v, page_table, P["lam4"], P["subln_w_c"],
                            lam_init, seq_len)
    x = _mm([attn], [P["w_out_c"]], [(0, 0)], D_MODEL, tm=tm, tn=1024, res=x, name="proj_out_c")

    h = _rms(x, P["norm_ffn"][1], BF16)
    gated, nfg1, nfv1 = _ffn_up(h, P["w_up_f"], 1, P["conv_w_f"][1], P["conv_b_f"][1], ffn_buf[1],
                                seq_len, tm, 512)
    x = _mm([gated], [P["w_down_f"][1]], [(0, 0)], D_MODEL, tm=tm, tn=512, res=x, name="ffn_down")

    y_out = _rms(x, P["norm_final"], F32)
    new_ffn = jnp.stack([jnp.concatenate([nfg0, nfv0], axis=-1), jnp.concatenate([nfg1, nfv1], axis=-1)])
    return (y_out.reshape(bsz, seq_len, D_MODEL),
            new_conv[None],
            new_shift[None, :, 0, :P_B],
            new_wkv[None],
            k_rows.reshape(1, bsz, seq_len, 2 * H_C, HD_C),
            v_rows.reshape(1, bsz, seq_len, H_C, DV_C),
            new_ffn)


def kernel(x_prompt, x_sample, state_conv_a, state_shift_b, state_wkv_b, cache_k, cache_v, page_table,
           state_ffn_conv, norm_mix, norm_ffn, norm_final, w_in_ab, conv_w_a, mu_b, w0_b, w2_b, a0_b, a2_b,
           g2_b, k_k_b, k_a_b, r_k_b, ln_gain_b, ln_bias_b, w_out_ab, w_qkv_c, lambda_q1_c, lambda_k1_c,
           lambda_q2_c, lambda_k2_c, subln_w_c, w_out_c, w_up_f, conv_w_f, conv_b_f, w_down_f):
    bp, seq, d = x_prompt.shape
    bs, dec_seq, _ = x_sample.shape
    row = lambda a: a.reshape(1, -1)
    lora_rows = jnp.zeros((LORA_PAD, D_B), F32)
    P = dict(
        norm_mix=norm_mix, norm_ffn=norm_ffn, norm_final=norm_final,
        w_in_a=w_in_ab[0, :, :P_A].astype(BF16),
        w_in_b=_pad_cols(w_in_ab[0, :, P_A:], P_B_PAD).astype(BF16),
        conv_w_a=conv_w_a[0],
        mu_b=_pad_cols(row(mu_b[0]), P_B_PAD),
        w0_b=row(w0_b[0]), a0_b=row(a0_b[0]), k_k_b=row(k_k_b[0]), k_a_b=row(k_a_b[0]),
        w2_b=lora_rows.at[:R_W].set(w2_b[0]).astype(BF16),
        a2_b=lora_rows.at[R_W:R_W + R_A].set(a2_b[0]).astype(BF16),
        g2_b=lora_rows.at[R_W + R_A:R_W + R_A + R_G].set(g2_b[0]).astype(BF16),
        r_k_b=row(r_k_b[0]), ln_gain_b=row(ln_gain_b[0]), ln_bias_b=row(ln_bias_b[0]),
        w_out_ab=w_out_ab[0].astype(BF16),
        w_qkv_c=w_qkv_c[0].astype(BF16),
        lam4=jnp.stack([lambda_q1_c[0], lambda_k1_c[0], lambda_q2_c[0], lambda_k2_c[0]]),
        subln_w_c=row(subln_w_c[0]),
        w_out_c=w_out_c[0].astype(BF16),
        w_up_f=w_up_f.astype(BF16), conv_w_f=conv_w_f, conv_b_f=conv_b_f[:, None, :],
        w_down_f=w_down_f.astype(BF16),
    )
    n_phys = cache_k.shape[1]
    outs_p = _trunk(
        x_prompt.reshape(bp * seq, d),
        jnp.zeros((bp, CONV_W - 1, D_A), F32),
        jnp.zeros((bp, 1, P_B_PAD), F32),
        jnp.zeros((bp, H_B, HD_B, HD_B), F32),
        jnp.zeros((DEPTH, bp, CONV_W - 1, 2 * D_FF), F32),
        None, None, None, P, seq)
    outs_s = _trunk(
        x_sample.reshape(bs * dec_seq, d),
        state_conv_a[0],
        _pad_cols(state_shift_b[0], P_B_PAD)[:, None, :],
        state_wkv_b[0],
        state_ffn_conv,
        cache_k.reshape(n_phys, PAGE_SIZE * 2 * H_C, HD_C),
        cache_v.reshape(n_phys, PAGE_SIZE, H_C, DV_C // LANES, LANES).swapaxes(2, 3).reshape(
            n_phys, PAGE_SIZE * H_C * DV_C // LANES, LANES),
        page_table, P, dec_seq)
    return (outs_p[0], outs_s[0]) + outs_p[1:] + outs_s[1:]
```

```python
import functools
import math

import jax
import jax.numpy as jnp
from jax import lax
from jax.experimental import pallas as pl
from jax.experimental.pallas import tpu as pltpu

F32 = jnp.float32
BF16 = jnp.bfloat16

D_MODEL = 2048
DEPTH = 2
PAGE_SIZE = 128
CONV_W = 3
NORM_EPS = 1e-6
D_A = D_MODEL // 2
P_A = 3 * D_A
D_B = D_MODEL // 2
HD_B = 64
H_B = D_B // HD_B
R_W = 64
R_A = 64
R_G = 160
P_B = 3 * D_B + R_W + R_A + R_G
LORA_PAD = 384
P_B_PAD = 3 * D_B + LORA_PAD
GN_EPS_B = 64e-5
HD_C = 128
H_C = D_MODEL // (2 * HD_C)
DV_C = 2 * HD_C
D_QK_C = 2 * H_C * HD_C
D_V_C = H_C * DV_C
ATTN_SCALE = HD_C ** -0.5
SUBLN_EPS = 1e-5
D_FF = 5632

LANES = 128
SUBLANES = 8
VMEM_LIMIT = 52 * 1024 * 1024
NEG_BIG = -1e30


def _params(sem):
    return pltpu.CompilerParams(dimension_semantics=sem, vmem_limit_bytes=VMEM_LIMIT)


def _rms_kernel(x_ref, g_ref, o_ref):
    x = x_ref[...]
    y = x * lax.rsqrt(jnp.mean(x * x, axis=-1, keepdims=True) + NORM_EPS)
    o_ref[...] = (y * g_ref[...]).astype(o_ref.dtype)


def _rms(x, g, out_dtype, tm=512):
    m, d = x.shape
    return pl.pallas_call(
        _rms_kernel,
        grid=(m // tm,),
        in_specs=[pl.BlockSpec((tm, d), lambda i: (i, 0)), pl.BlockSpec((1, d), lambda i: (0, 0))],
        out_specs=pl.BlockSpec((tm, d), lambda i: (i, 0)),
        out_shape=jax.ShapeDtypeStruct((m, d), out_dtype),
        compiler_params=_params(("parallel",)),
        name="rmsnorm",
    )(x, g.reshape(1, d))


def _mm_kernel(*refs, nx, has_res, scale):
    xs, ws = refs[:nx], refs[nx:2 * nx]
    res = refs[2 * nx] if has_res else None
    outs = refs[2 * nx + int(has_res):]
    acc = None
    for x, w in zip(xs, ws):
        d = jnp.dot(x[...].astype(BF16), w[...], preferred_element_type=F32)
        acc = d if acc is None else acc + d
    if scale is not None:
        acc = acc * scale
    if res is not None:
        acc = acc + res[...]
    for o in outs:
        o[...] = acc.astype(o.dtype)


def _mm(xs, ws, wblocks, n, *, tm, tn, out_dtypes=(F32,), res=None, scale=None, name="matmul"):
    m = xs[0].shape[0]
    in_specs, args = [], []
    for x in xs:
        in_specs.append(pl.BlockSpec((tm, x.shape[1]), lambda i, j: (i, 0)))
        args.append(x)
    for x, w, (rb, cb) in zip(xs, ws, wblocks):
        in_specs.append(pl.BlockSpec((x.shape[1], tn), lambda i, j, rb=rb, cb=cb: (rb, cb + j)))
        args.append(w)
    if res is not None:
        in_specs.append(pl.BlockSpec((tm, tn), lambda i, j: (i, j)))
        args.append(res)
    outs = pl.pallas_call(
        functools.partial(_mm_kernel, nx=len(xs), has_res=res is not None, scale=scale),
        grid=(m // tm, n // tn),
        in_specs=in_specs,
        out_specs=[pl.BlockSpec((tm, tn), lambda i, j: (i, j)) for _ in out_dtypes],
        out_shape=[jax.ShapeDtypeStruct((m, n), dt) for dt in out_dtypes],
        compiler_params=_params(("parallel", "arbitrary")),
        name=name,
    )(*args)
    return outs if len(out_dtypes) > 1 else outs[0]


def _seq_tiles(tm, seq_len):
    return (tm // seq_len, seq_len) if seq_len < tm else (1, tm)


def _shift_rows(u3, prev, k):
    out = pltpu.roll(u3, k, 1)
    head = out[:, :SUBLANES, :]
    t = lax.broadcasted_iota(jnp.int32, head.shape, 1)
    for s in range(k):
        head = jnp.where(t == s, prev[:, 2 - k + s:3 - k + s, :], head)
    if u3.shape[1] == SUBLANES:
        return head
    return jnp.concatenate([head, out[:, SUBLANES:, :]], axis=1)


def _conv3(u3, prev, w):
    return w[0:1, :] * _shift_rows(u3, prev, 2) + w[1:2, :] * _shift_rows(u3, prev, 1) + w[2:3, :] * u3


def _mixer_a_kernel(gb_ref, gc_ref, xt_ref, w_ref, buf_ref, ya_ref, nc_ref, carry_ref, *, nb, tt, carried):
    c = gb_ref.shape[-1]
    cx = (gc_ref[...] * xt_ref[...]).reshape(nb, tt, c)
    if carried:
        @pl.when(pl.program_id(0) == 0)
        def _():
            carry_ref[...] = buf_ref[...]
        prev = carry_ref[...]
    else:
        prev = buf_ref[...]
    conv = _conv3(cx, prev, w_ref[...])
    ya_ref[...] = (gb_ref[...] * conv.reshape(nb * tt, c)).astype(ya_ref.dtype)
    tail = cx[:, tt - 2:tt, :]
    nc_ref[...] = tail
    if carried:
        carry_ref[...] = tail


def _mixer_a(pa, conv_w, buf, seq_len, tm):
    m = pa.shape[0]
    nb, tt = _seq_tiles(tm, seq_len)
    carried = nb == 1
    bsz = buf.shape[0]
    buf_spec = (pl.BlockSpec((1, 2, D_A), lambda i: (0, 0, 0)) if carried
                else pl.BlockSpec((nb, 2, D_A), lambda i: (i, 0, 0)))
    ya, nc = pl.pallas_call(
        functools.partial(_mixer_a_kernel, nb=nb, tt=tt, carried=carried),
        grid=(m // tm,),
        in_specs=[pl.BlockSpec((tm, D_A), lambda i: (i, 0)),
                  pl.BlockSpec((tm, D_A), lambda i: (i, 1)),
                  pl.BlockSpec((tm, D_A), lambda i: (i, 2)),
                  pl.BlockSpec((CONV_W, D_A), lambda i: (0, 0)),
                  buf_spec],
        out_specs=[pl.BlockSpec((tm, D_A), lambda i: (i, 0)), buf_spec],
        out_shape=[jax.ShapeDtypeStruct((m, D_A), BF16), jax.ShapeDtypeStruct((bsz, 2, D_A), F32)],
        scratch_shapes=[pltpu.VMEM((1, 2, D_A), F32)],
        compiler_params=_params(("arbitrary",)),
        name="mixer_a",
    )(pa, pa, pa, conv_w, buf)
    return ya, nc


def _split_bf16(x):
    hi = x.astype(BF16)
    lo = (x - hi.astype(F32)).astype(BF16)
    return hi, lo


def _head_sum(x):
    blk = 2 * LANES
    r = lax.broadcasted_iota(jnp.int32, (blk, blk), 0) // HD_B
    c = lax.broadcasted_iota(jnp.int32, (blk, blk), 1) // HD_B
    ones = jnp.where(r == c, 1.0, 0.0).astype(BF16)
    hi, lo = _split_bf16(x)
    parts = []
    for s in range(x.shape[1] // blk):
        sl = slice(s * blk, (s + 1) * blk)
        parts.append(jnp.dot(hi[:, sl], ones, preferred_element_type=F32)
                     + jnp.dot(lo[:, sl], ones, preferred_element_type=F32))
    return jnp.concatenate(parts, axis=1)


def _prep_b_kernel(pb_ref, prev_ref, mu_ref, w0_ref, a0_ref, kk_ref, ka_ref, w2_ref, a2_ref, g2_ref,
                   r_out, lw_out, k_out, v_out, kk_out, b_out, g_out, ns_out,
                   carry_ref, *, nb, tt, carried):
    c = pb_ref.shape[-1]
    pb3 = pb_ref[...].reshape(nb, tt, c)
    if carried:
        @pl.when(pl.program_id(0) == 0)
        def _():
            carry_ref[...] = prev_ref[...]
        prev = carry_ref[...]
    else:
        prev = prev_ref[...]
    t = lax.broadcasted_iota(jnp.int32, pb3.shape, 1)
    shifted = jnp.where(t == 0, prev, pltpu.roll(pb3, 1, 1))
    last = pb3[:, tt - 1:tt, :]
    ns_out[...] = last
    if carried:
        carry_ref[...] = last
    x = (pb3 + mu_ref[...] * (shifted - pb3)).reshape(nb * tt, c)
    r = x[:, :D_B]
    k = x[:, D_B:2 * D_B]
    v = x[:, 2 * D_B:3 * D_B]
    lo = x[:, 3 * D_B:]
    wl = w0_ref[...] + jnp.dot(jnp.tanh(lo).astype(BF16), w2_ref[...], preferred_element_type=F32)
    softplus_neg = jnp.maximum(-wl, 0.0) + jnp.log(1.0 + jnp.exp(-jnp.abs(wl)))
    w_log = -softplus_neg - 0.5
    a = jax.nn.sigmoid(a0_ref[...] + jnp.dot(lo.astype(BF16), a2_ref[...], preferred_element_type=F32))
    g = jnp.dot(jax.nn.sigmoid(lo).astype(BF16), g2_ref[...], preferred_element_type=F32)
    kk = k * kk_ref[...]
    kk = kk / jnp.maximum(jnp.sqrt(_head_sum(kk * kk)), 1e-12)
    k2 = k * (1.0 + (a - 1.0) * ka_ref[...])
    b = kk * a
    r_out[...] = r
    lw_out[...] = -jnp.exp(w_log)
    k_out[...] = k2
    v_out[...] = v
    kk_out[...] = kk
    b_out[...] = b
    g_out[...] = g


def _prep_b(pb, shift_prev, mu, w0, a0, k_k, k_a, w2p, a2p, g2p, seq_len, tm):
    m = pb.shape[0]
    nb, tt = _seq_tiles(tm, seq_len)
    carried = nb == 1
    bsz = shift_prev.shape[0]
    prev_spec = (pl.BlockSpec((1, 1, P_B_PAD), lambda i: (0, 0, 0)) if carried
                 else pl.BlockSpec((nb, 1, P_B_PAD), lambda i: (i, 0, 0)))
    row = lambda n: pl.BlockSpec((1, n), lambda i: (0, 0))
    lora = pl.BlockSpec((LORA_PAD, D_B), lambda i: (0, 0))
    tile = pl.BlockSpec((tm, D_B), lambda i: (i, 0))
    outs = pl.pallas_call(
        functools.partial(_prep_b_kernel, nb=nb, tt=tt, carried=carried),
        grid=(m // tm,),
        in_specs=[pl.BlockSpec((tm, P_B_PAD), lambda i: (i, 0)), prev_spec, row(P_B_PAD),
                  row(D_B), row(D_B), row(D_B), row(D_B), lora, lora, lora],
        out_specs=[tile] * 7 + [prev_spec],
        out_shape=[jax.ShapeDtypeStruct((m, D_B), F32)] * 7 + [jax.ShapeDtypeStruct((bsz, 1, P_B_PAD), F32)],
        scratch_shapes=[pltpu.VMEM((1, 1, P_B_PAD), F32)],
        compiler_params=_params(("arbitrary",)),
        name="rwkv_prep",
    )(pb, shift_prev, mu, w0, a0, k_k, k_a, w2p, a2p, g2p)
    return outs


def _split3_bf16(x):
    hi = x.astype(BF16)
    rest = x - hi.astype(F32)
    mid = rest.astype(BF16)
    return hi, mid, (rest - mid.astype(F32)).astype(BF16)


def _dot_bf16(a, b, dims=(((1,), (0,)), ((), ()))):
    return lax.dot_general(a.astype(BF16), b.astype(BF16), dims, preferred_element_type=F32)


_NT = (((1,), (1,)), ((), ()))
_TN = (((0,), (0,)), ((), ()))
WKV_CHUNK = 128
WKV_SHORT_NSEQ = 8


def _wkv_chunk_kernel(r_ref, lw_ref, k_ref, v_ref, kk_ref, b_ref, s0_ref, y_ref, sf_ref, s_scr, *, c, nseq):
    ci = pl.program_id(1)
    npair = H_B // 2
    rows = nseq * c

    @pl.when(ci == 0)
    def _():
        s_scr[...] = jnp.zeros_like(s_scr)
        for q in range(nseq):
            for j in range(npair):
                s_scr[q * npair + j, 0:HD_B, 0:HD_B] = s0_ref[q, 2 * j]
                s_scr[q * npair + j, HD_B:, HD_B:] = s0_ref[q, 2 * j + 1]

    ti = lax.broadcasted_iota(jnp.int32, (rows, rows), 0)
    tj = lax.broadcasted_iota(jnp.int32, (rows, rows), 1)
    tri = jnp.where(jnp.logical_and(ti // c == tj // c, tj <= ti), 1.0, 0.0).astype(BF16)
    lw = lw_ref[...]
    cum = None
    for limb in _split3_bf16(lw):
        d = jnp.dot(tri, limb, preferred_element_type=F32)
        cum = d if cum is None else cum + d
    cum3 = cum.reshape(nseq, c, D_B)
    last3 = cum3[:, c - 1:c, :]
    mid3 = cum3[:, c // 2 - 1:c // 2, :]
    cc = (cum3 - mid3).reshape(rows, D_B)
    g_mid3 = jnp.exp(mid3)
    g_last3 = jnp.exp(last3)
    g_inv = jnp.exp(-cc)
    kt = kk_ref[...] * jnp.exp(cc - lw)
    rt = r_ref[...] * jnp.exp(cc)
    bt = b_ref[...] * g_inv
    kq = k_ref[...] * g_inv
    to_end = jnp.exp(last3 - cum3).reshape(rows, D_B)
    kh = k_ref[...] * to_end
    bh = b_ref[...] * to_end
    v = v_ref[...]

    ci_ = lax.broadcasted_iota(jnp.int32, (c, c), 0)
    cj_ = lax.broadcasted_iota(jnp.int32, (c, c), 1)
    first = lax.broadcasted_iota(jnp.int32, (c, LANES), 1) < HD_B
    col2 = lax.broadcasted_iota(jnp.int32, (c, 2 * c), 1)
    row2 = lax.broadcasted_iota(jnp.int32, (c, 2 * c), 0)
    strict = (col2 % c) < row2
    incl = (col2 % c) <= row2
    second_grp = col2 >= c
    strict_sq = cj_ < ci_
    blockdiag = (lax.broadcasted_iota(jnp.int32, (LANES, LANES), 0) // HD_B
                 == lax.broadcasted_iota(jnp.int32, (LANES, LANES), 1) // HD_B)
    lane_l = lax.broadcasted_iota(jnp.int32, (2 * c, LANES), 1)
    nsq = (c - 1).bit_length() - 1

    units = [(q, j) for q in range(nseq) for j in range(npair)]
    rsl = [slice(q * c, (q + 1) * c) for q, j in units]
    sls = [slice(j * LANES, (j + 1) * LANES) for q, j in units]
    s2s = [s_scr[q * npair + j] for q, j in units]
    lhs2s = [jnp.concatenate([kt[rs, sl], rt[rs, sl]], axis=0) for rs, sl in zip(rsl, sls)]
    rhs2s = [jnp.concatenate([bt[rs, sl], kq[rs, sl]], axis=0) for rs, sl in zip(rsl, sls)]
    vps = [v[rs, sl] for rs, sl in zip(rsl, sls)]
    wy0s = [_dot_bf16(lhs2s[n], s2s[n] * g_mid3[q][:, sls[n]], _NT)
            for n, (q, j) in enumerate(units)]
    ns, rs_, xs = [], [], []
    for n in range(len(units)):
        for hh in range(2):
            mine = (lane_l < HD_B) if hh == 0 else (lane_l >= HD_B)
            m1 = _dot_bf16(jnp.where(mine, lhs2s[n], 0.0), rhs2s[n], _NT)
            ab_ak = jnp.where(strict, m1[:c], 0.0)
            rs_.append(jnp.where(incl, m1[c:], 0.0))
            ns.append(jnp.where(strict_sq, m1[:c, :c], 0.0))
            xs.append(wy0s[n][:c] + _dot_bf16(jnp.where(second_grp, ab_ak, 0.0),
                                              jnp.concatenate([vps[n], vps[n]], axis=0)))
    xs = [x - _dot_bf16(nm, x) for nm, x in zip(ns, xs)]
    ps = ns
    for _ in range(nsq):
        ps = [_dot_bf16(p, p) for p in ps]
        xs = [x + _dot_bf16(p, x) for p, x in zip(ps, xs)]
    for n, (q, j) in enumerate(units):
        u = jnp.where(first, xs[2 * n], xs[2 * n + 1])
        stack = jnp.concatenate([-u, vps[n]], axis=0)
        y_ref[rsl[n], sls[n]] = wy0s[n][c:] + jnp.where(first, _dot_bf16(rs_[2 * n], stack),
                                                        _dot_bf16(rs_[2 * n + 1], stack))
        ds = _dot_bf16(jnp.concatenate([vps[n], -u], axis=0),
                       jnp.concatenate([kh[rsl[n], sls[n]], bh[rsl[n], sls[n]]], axis=0), _TN)
        s_scr[q * npair + j] = s2s[n] * g_last3[q][:, sls[n]] + jnp.where(blockdiag, ds, 0.0)

    @pl.when(ci == pl.num_programs(1) - 1)
    def _():
        for q in range(nseq):
            for j in range(npair):
                s2 = s_scr[q * npair + j]
                sf_ref[q, 2 * j] = s2[0:HD_B, 0:HD_B]
                sf_ref[q, 2 * j + 1] = s2[HD_B:, HD_B:]


def _wkv_chunked(r, lw, k, v, kk, b, s0, seq_len, c, nseq=1):
    bsz = s0.shape[0]
    nc = seq_len // c
    assert nseq == 1 or nc == 1
    tile = pl.BlockSpec((nseq * c, D_B), lambda bi, ci: (bi * nc + ci, 0))
    st = pl.BlockSpec((nseq, H_B, HD_B, HD_B), lambda bi, ci: (bi, 0, 0, 0))
    return pl.pallas_call(
        functools.partial(_wkv_chunk_kernel, c=c, nseq=nseq),
        grid=(bsz // nseq, nc),
        in_specs=[tile] * 6 + [st],
        out_specs=[tile, st],
        out_shape=[jax.ShapeDtypeStruct((bsz * seq_len, D_B), F32),
                   jax.ShapeDtypeStruct((bsz, H_B, HD_B, HD_B), F32)],
        scratch_shapes=[pltpu.VMEM((nseq * H_B // 2, LANES, LANES), F32)],
        compiler_params=_params(("parallel", "arbitrary")),
        name="wkv_chunked",
    )(r, lw, k, v, kk, b, s0)


def _post_b_kernel(y_ref, r_ref, k_ref, v_ref, g_ref, gain_ref, bias_ref, rk_ref, o_ref):
    y = y_ref[...]
    d = y - _head_sum(y) * (1.0 / HD_B)
    var = _head_sum(d * d) * (1.0 / HD_B)
    yn = d * lax.rsqrt(var + GN_EPS_B) * gain_ref[...] + bias_ref[...]
    v = v_ref[...]
    bonus = _head_sum(r_ref[...] * k_ref[...] * rk_ref[...]) * v
    o_ref[...] = ((yn + bonus) * g_ref[...]).astype(o_ref.dtype)


def _post_b(y, r, k, v, g, gain, bias, r_k, tm):
    m = y.shape[0]
    tile = pl.BlockSpec((tm, D_B), lambda i: (i, 0))
    row = pl.BlockSpec((1, D_B), lambda i: (0, 0))
    return pl.pallas_call(
        _post_b_kernel,
        grid=(m // tm,),
        in_specs=[tile] * 5 + [row] * 3,
        out_specs=tile,
        out_shape=jax.ShapeDtypeStruct((m, D_B), BF16),
        compiler_params=_params(("parallel",)),
        name="rwkv_post",
    )(y, r, k, v, g, gain, bias, r_k)


def _lambda(lam_ref, lam_init):
    l4 = lam_ref[...]
    s1 = jnp.sum(l4[0:1, :] * l4[1:2, :], axis=1, keepdims=True)
    s2 = jnp.sum(l4[2:3, :] * l4[3:4, :], axis=1, keepdims=True)
    return jnp.exp(s1) - jnp.exp(s2) + lam_init


def _diff_combine(o0, o1, lam, subw, lam_init):
    o = o0 - lam * o1
    o = o * lax.rsqrt(jnp.mean(o * o, axis=-1, keepdims=True) + SUBLN_EPS) * subw
    return o * (1.0 - lam_init)


def _flash_kernel(q_ref, k_ref, v_ref, lam_ref, subw_ref, o_ref, m_scr, l_scr, acc_scr, *, tq, tk, lam_init):
    qi = pl.program_id(1)
    m_scr[...] = jnp.full_like(m_scr, NEG_BIG)
    l_scr[...] = jnp.zeros_like(l_scr)
    acc_scr[...] = jnp.zeros_like(acc_scr)

    def block(off, diag):
        kb = k_ref[pl.ds(off, tk), :]
        vb = v_ref[pl.ds(off, tk), :]
        r0 = 0 if diag is None else diag * tk
        rows = slice(r0, tq)
        if diag is not None:
            row = lax.broadcasted_iota(jnp.int32, (tq - r0, tk), 0)
            col = lax.broadcasted_iota(jnp.int32, (tq - r0, tk), 1)
            keep = col <= row
        for i in range(2):
            sl = slice(i * HD_C, (i + 1) * HD_C)
            s = lax.dot_general(q_ref[rows, sl], kb[:, sl], (((1,), (1,)), ((), ())),
                                preferred_element_type=F32)
            if diag is not None:
                s = jnp.where(keep, s, NEG_BIG)
            m_old = m_scr[i, rows, :]
            m_new = jnp.maximum(m_old, jnp.max(s, axis=1, keepdims=True))
            corr = jnp.exp2(m_old - m_new)
            p = jnp.exp2(s - jnp.concatenate([m_new] * (tk // LANES), axis=1))
            l_scr[i, rows, :] = l_scr[i, rows, :] * corr + jnp.sum(p, axis=1, keepdims=True)
            acc_scr[i, rows, :] = (acc_scr[i, rows, :] * jnp.concatenate([corr] * (DV_C // LANES), axis=1)
                                   + jnp.dot(p.astype(BF16), vb, preferred_element_type=F32))
            m_scr[i, rows, :] = m_new

    def full_block(j, carry):
        block(pl.multiple_of(j * tk, tk), None)
        return carry

    lax.fori_loop(0, (qi * tq) // tk, full_block, 0)
    for d in range(tq // tk):
        block(pl.multiple_of(qi * tq + d * tk, tk), d)

    lam = _lambda(lam_ref, lam_init)
    o0 = acc_scr[0] / jnp.concatenate([l_scr[0]] * (DV_C // LANES), axis=1)
    o1 = acc_scr[1] / jnp.concatenate([l_scr[1]] * (DV_C // LANES), axis=1)
    o_ref[...] = _diff_combine(o0, o1, lam, subw_ref[...], lam_init).astype(o_ref.dtype)


def _flash_prompt(q, k, v, lam4, subw, lam_init, tq, tk):
    s = q.shape[0]
    blk = 2 * HD_C
    return pl.pallas_call(
        functools.partial(_flash_kernel, tq=tq, tk=tk, lam_init=lam_init),
        grid=(H_C, s // tq),
        in_specs=[pl.BlockSpec((tq, blk), lambda h, i: (i, h)),
                  pl.BlockSpec((s, blk), lambda h, i: (0, h)),
                  pl.BlockSpec((s, DV_C), lambda h, i: (0, h)),
                  pl.BlockSpec((4, HD_C), lambda h, i: (0, 0)),
                  pl.BlockSpec((1, DV_C), lambda h, i: (0, 0))],
        out_specs=pl.BlockSpec((tq, DV_C), lambda h, i: (i, h)),
        out_shape=jax.ShapeDtypeStruct((s, D_V_C), BF16),
        scratch_shapes=[pltpu.VMEM((2, tq, LANES), F32), pltpu.VMEM((2, tq, LANES), F32),
                        pltpu.VMEM((2, tq, DV_C), F32)],
        compiler_params=_params(("parallel", "arbitrary")),
        name="diff_attn_prompt",
    )(q, k, v, lam4, subw)


DECODE_PAGES = 8


def _decode_kernel(pt_ref, q_ref, kn_ref, vn_ref, *rest, steps, lam_init):
    del pt_ref
    kc_refs, vc_refs = rest[:DECODE_PAGES], rest[DECODE_PAGES:2 * DECODE_PAGES]
    lam_ref, subw_ref, o_ref, qx_scr, qrow_scr, m_scr, l_scr, acc_scr = rest[2 * DECODE_PAGES:]
    pj = pl.program_id(1)
    nrow = 2 * H_C * steps
    grp = 2 * steps
    npseudo = 2 * PAGE_SIZE
    half = H_C * HD_C

    @pl.when(pj == 0)
    def _():
        q = jnp.tile(q_ref[...], (2 * H_C, 1))
        r = lax.broadcasted_iota(jnp.int32, q.shape, 0) // steps
        c = lax.broadcasted_iota(jnp.int32, q.shape, 1) // HD_C
        qrow_scr[...] = jnp.where(r == c, q, 0.0).astype(BF16)
        rx = lax.broadcasted_iota(jnp.int32, (nrow, half), 0) // steps
        cx = lax.broadcasted_iota(jnp.int32, (nrow, half), 1) // HD_C
        folded = jnp.where(rx < H_C, q[:, :half], q[:, half:])
        qx_scr[...] = jnp.where(rx % H_C == cx, folded, 0.0).astype(BF16)
        m_scr[...] = jnp.full_like(m_scr, NEG_BIG)
        l_scr[...] = jnp.zeros_like(l_scr)
        acc_scr[...] = jnp.zeros_like(acc_scr)

    def softmax_step(s):
        m_old = m_scr[...]
        m_new = jnp.maximum(m_old, jnp.max(s, axis=1, keepdims=True))
        corr = jnp.exp(m_old - m_new)
        p = jnp.exp(s - m_new)
        l_scr[...] = l_scr[...] * corr + jnp.sum(p, axis=1, keepdims=True)
        m_scr[...] = m_new
        return p, corr

    kps = jnp.concatenate(
        [jnp.concatenate([ref[pl.ds(h8, npseudo, stride=H_C), :].astype(BF16) for h8 in range(H_C)], axis=1)
         for ref in kc_refs], axis=0)
    s = lax.dot_general(qx_scr[...], kps, (((1,), (1,)), ((), ())), preferred_element_type=F32)
    ncol = s.shape[1]
    col_par = lax.broadcasted_iota(jnp.int32, (nrow, ncol), 1) % 2
    row_par = lax.broadcasted_iota(jnp.int32, (nrow, ncol), 0) // (H_C * steps)
    p, corr = softmax_step(jnp.where(col_par == row_par, s, NEG_BIG))
    for h in range(H_C):
        rows = slice(h * grp, (h + 1) * grp)
        par = (2 * h) // H_C
        ph = p[rows, :]
        other = pltpu.roll(ph, 1 if par == 0 else ncol - 1, 1)
        vps = jnp.concatenate([ref[pl.ds(h, npseudo, stride=H_C), :].astype(BF16) for ref in vc_refs], axis=0)
        pv = jnp.dot(jnp.concatenate([ph, other], axis=0).astype(BF16), vps, preferred_element_type=F32)
        mine, theirs = pv[:grp], pv[grp:]
        both = jnp.concatenate([mine, theirs] if par == 0 else [theirs, mine], axis=1)
        acc_scr[rows, :] = acc_scr[rows, :] * corr[rows, :] + both

    @pl.when(pj == pl.num_programs(1) - 1)
    def _():
        pad = jnp.zeros((PAGE_SIZE - steps, kn_ref.shape[1]), F32)
        kn = jnp.concatenate([kn_ref[...], pad], axis=0).astype(BF16)
        vn = jnp.concatenate([vn_ref[...], pad], axis=0).astype(BF16)
        sn = lax.dot_general(qrow_scr[...], kn, (((1,), (1,)), ((), ())), preferred_element_type=F32)
        step = lax.broadcasted_iota(jnp.int32, (nrow, PAGE_SIZE), 0) % steps
        key = lax.broadcasted_iota(jnp.int32, (nrow, PAGE_SIZE), 1)
        pn, cn = softmax_step(jnp.where(key <= step, sn, NEG_BIG))
        pn = pn.astype(BF16)
        lam = _lambda(lam_ref, lam_init)
        linv = 1.0 / l_scr[...]
        for h in range(H_C):
            rows = slice(h * grp, (h + 1) * grp)
            o = (acc_scr[rows, :] * cn[rows, :] + jnp.dot(pn[rows, :], vn[:, h * DV_C:(h + 1) * DV_C],
                                                          preferred_element_type=F32)) * linv[rows, :]
            o_ref[:, h * DV_C:(h + 1) * DV_C] = _diff_combine(o[:steps], o[steps:], lam, subw_ref[...], lam_init)


def _decode_attn(q, kn, vn, cache_k, cache_v, page_table, lam4, subw, lam_init, steps):
    bsz, n_pages = page_table.shape
    nrow = 2 * H_C * steps
    tile = lambda n: pl.BlockSpec((steps, n), lambda b, p, pt: (b, 0))

    def page(shape, u):
        return pl.BlockSpec((None,) + shape,
                            lambda b, p, pt: (pt[b * n_pages + p * DECODE_PAGES + u],) + (0,) * len(shape))

    grid_spec = pltpu.PrefetchScalarGridSpec(
        num_scalar_prefetch=1,
        grid=(bsz, n_pages // DECODE_PAGES),
        in_specs=([tile(D_QK_C), tile(D_QK_C), tile(D_V_C)]
                  + [page((PAGE_SIZE * 2 * H_C, HD_C), u) for u in range(DECODE_PAGES)]
                  + [page((PAGE_SIZE * H_C * DV_C // LANES, LANES), u) for u in range(DECODE_PAGES)]
                  + [pl.BlockSpec((4, HD_C), lambda b, p, pt: (0, 0)),
                     pl.BlockSpec((1, DV_C), lambda b, p, pt: (0, 0))]),
        out_specs=tile(D_V_C),
        scratch_shapes=[pltpu.VMEM((nrow, H_C * HD_C), BF16), pltpu.VMEM((nrow, D_QK_C), BF16),
                        pltpu.VMEM((nrow, 1), F32), pltpu.VMEM((nrow, 1), F32), pltpu.VMEM((nrow, DV_C), F32)],
    )
    return pl.pallas_call(
        functools.partial(_decode_kernel, steps=steps, lam_init=lam_init),
        grid_spec=grid_spec,
        out_shape=jax.ShapeDtypeStruct((bsz * steps, D_V_C), F32),
        compiler_params=_params(("parallel", "arbitrary")),
        name="diff_attn_decode",
    )(page_table.reshape(-1), q, kn, vn, *([cache_k] * DECODE_PAGES), *([cache_v] * DECODE_PAGES), lam4, subw)


def _ffn_up_short_kernel(h_ref, wg_ref, wv_ref, cwg_ref, cwv_ref, cbg_ref, cbv_ref, bg_ref, bv_ref,
                         o_ref, ng_ref, nv_ref, *, nb, tt):
    c = wg_ref.shape[-1]
    h = h_ref[...]

    def branch(w_ref, cw_ref, cb_ref, buf_ref, new_ref):
        u = jnp.dot(h, w_ref[...].astype(BF16), preferred_element_type=F32).reshape(nb, tt, c)
        y = _conv3(u, buf_ref[...], cw_ref[...]) + cb_ref[...]
        new_ref[...] = u[:, tt - 2:tt, :]
        return y.reshape(nb * tt, c)

    gate = branch(wg_ref, cwg_ref, cbg_ref, bg_ref, ng_ref)
    val = branch(wv_ref, cwv_ref, cbv_ref, bv_ref, nv_ref)
    o_ref[...] = (jax.nn.silu(gate) * val).astype(o_ref.dtype)


def _ffn_up_long_kernel(h_ref, wg_ref, wv_ref, cwg_ref, cwv_ref, cbg_ref, cbv_ref, bg_ref, bv_ref,
                        o_ref, ng_ref, nv_ref, cg_scr, cv_scr, wg_scr, wv_scr, *, tm):
    c = wg_ref.shape[-1]

    @pl.when(pl.program_id(1) == 0)
    def _():
        cg_scr[...] = bg_ref[...]
        cv_scr[...] = bv_ref[...]
        wg_scr[...] = wg_ref[...].astype(BF16)
        wv_scr[...] = wv_ref[...].astype(BF16)

    h = h_ref[...]

    def branch(w_ref, cw_ref, cb_ref, new_ref, carry_ref):
        u = jnp.dot(h, w_ref[...], preferred_element_type=F32).reshape(1, tm, c)
        y = _conv3(u, carry_ref[...], cw_ref[...]) + cb_ref[...]
        tail = u[:, tm - 2:tm, :]
        new_ref[...] = tail
        carry_ref[...] = tail
        return y.reshape(tm, c)

    gate = branch(wg_scr, cwg_ref, cbg_ref, ng_ref, cg_scr)
    val = branch(wv_scr, cwv_ref, cbv_ref, nv_ref, cv_scr)
    o_ref[...] = (jax.nn.silu(gate) * val).astype(o_ref.dtype)


def _ffn_up(h, w_up, layer, conv_w, conv_b, buf, seq_len, tm, tn):
    m, d = h.shape
    nj = D_FF // tn
    nb, tt = _seq_tiles(tm, seq_len)
    carried = nb == 1
    bsz = buf.shape[0]

    def cols(off):
        return lambda j, i: (0, j + off)

    def wcols(off):
        return lambda j, i: (layer, 0, j + off)

    def buf_spec(off):
        if carried:
            return pl.BlockSpec((1, 2, tn), lambda j, i: (0, 0, j + off))
        return pl.BlockSpec((nb, 2, tn), lambda j, i: (i, 0, j + off))

    if carried:
        body = functools.partial(_ffn_up_long_kernel, tm=tm)
        scratch = [pltpu.VMEM((1, 2, tn), F32), pltpu.VMEM((1, 2, tn), F32),
                   pltpu.VMEM((d, tn), BF16), pltpu.VMEM((d, tn), BF16)]
    else:
        body = functools.partial(_ffn_up_short_kernel, nb=nb, tt=tt)
        scratch = []
    return pl.pallas_call(
        body,
        grid=(nj, m // tm),
        in_specs=[pl.BlockSpec((tm, d), lambda j, i: (i, 0)),
                  pl.BlockSpec((None, d, tn), wcols(0)), pl.BlockSpec((None, d, tn), wcols(nj)),
                  pl.BlockSpec((CONV_W, tn), cols(0)), pl.BlockSpec((CONV_W, tn), cols(nj)),
                  pl.BlockSpec((1, tn), cols(0)), pl.BlockSpec((1, tn), cols(nj)),
                  buf_spec(0), buf_spec(nj)],
        out_specs=[pl.BlockSpec((tm, tn), lambda j, i: (i, j)), buf_spec(0), buf_spec(0)],
        out_shape=[jax.ShapeDtypeStruct((m, D_FF), BF16),
                   jax.ShapeDtypeStruct((bsz, 2, D_FF), F32), jax.ShapeDtypeStruct((bsz, 2, D_FF), F32)],
        scratch_shapes=scratch,
        compiler_params=_params(("parallel", "arbitrary")),
        name="ffn_up",
    )(h, w_up, w_up, conv_w, conv_w, conv_b, conv_b, buf, buf)


def _pad_cols(x, n):
    return jnp.pad(x, [(0, 0)] * (x.ndim - 1) + [(0, n - x.shape[-1])])


def _trunk(x, conv_a, shift_b, wkv_b, ffn_buf, cache_k, cache_v, page_table, P, seq_len):
    m = x.shape[0]
    bsz = m // seq_len
    tm = 1024
    tm_small = 256

    h = _rms(x, P["norm_mix"][0], BF16)
    pa = _mm([h], [P["w_in_a"]], [(0, 0)], P_A, tm=tm, tn=1024, name="proj_in_a")
    pb = _mm([h], [P["w_in_b"]], [(0, 0)], P_B_PAD, tm=tm, tn=P_B_PAD // 3, name="proj_in_b")
    y_a, new_conv = _mixer_a(pa, P["conv_w_a"], conv_a, seq_len, 512)
    (r, lw, k2, v, kk, b, g, new_shift) = _prep_b(
        pb, shift_b, P["mu_b"], P["w0_b"], P["a0_b"], P["k_k_b"], P["k_a_b"],
        P["w2_b"], P["a2_b"], P["g2_b"], seq_len, tm_small)
    if seq_len % WKV_CHUNK == 0:
        y, new_wkv = _wkv_chunked(r, lw, k2, v, kk, b, wkv_b, seq_len, WKV_CHUNK)
    else:
        y, new_wkv = _wkv_chunked(r, lw, k2, v, kk, b, wkv_b, seq_len, seq_len, nseq=WKV_SHORT_NSEQ)
    y_b = _post_b(y, r, k2, v, g, P["ln_gain_b"], P["ln_bias_b"], P["r_k_b"], tm_small)
    x = _mm([y_a, y_b], [P["w_out_ab"], P["w_out_ab"]], [(0, 0), (1, 0)], D_MODEL,
            tm=tm, tn=1024, res=x, name="proj_out_ab")

    h = _rms(x, P["norm_ffn"][0], BF16)
    gated, nfg0, nfv0 = _ffn_up(h, P["w_up_f"], 0, P["conv_w_f"][0], P["conv_b_f"][0], ffn_buf[0],
                                seq_len, tm, 512)
    x = _mm([gated], [P["w_down_f"][0]], [(0, 0)], D_MODEL, tm=tm, tn=512, res=x, name="ffn_down")

    lam_init = 0.8 - 0.6 * math.exp(-0.3 * 1)
    h = _rms(x, P["norm_mix"][1], BF16)
    nq = D_QK_C // 1024
    if cache_k is None:
        qh = _mm([h], [P["w_qkv_c"]], [(0, 0)], D_QK_C, tm=tm, tn=1024, out_dtypes=(BF16,),
                 scale=ATTN_SCALE * math.log2(math.e), name="proj_q")
        k_rows, k_bf = _mm([h], [P["w_qkv_c"]], [(0, nq)], D_QK_C, tm=tm, tn=1024,
                           out_dtypes=(F32, BF16), name="proj_k")
        v_rows, v_bf = _mm([h], [P["w_qkv_c"]], [(0, 2 * nq)], D_V_C, tm=tm, tn=1024,
                           out_dtypes=(F32, BF16), name="proj_v")
        attn = _flash_prompt(qh, k_bf, v_bf, P["lam4"], P["subln_w_c"], lam_init, 1024, 512)
    else:
        qh = _mm([h], [P["w_qkv_c"]], [(0, 0)], D_QK_C, tm=tm, tn=1024, scale=ATTN_SCALE, name="proj_q")
        k_rows = _mm([h], [P["w_qkv_c"]], [(0, nq)], D_QK_C, tm=tm, tn=1024, name="proj_k")
        v_rows = _mm([h], [P["w_qkv_c"]], [(0, 2 * nq)], D_V_C, tm=tm, tn=1024, name="proj_v")
        attn = _decode_attn(qh, k_rows, v_rows, cache_k, cache_v, page_table, P["lam4"], P["subln_w_c"],
                            lam_init, seq_len)
    x = _mm([attn], [P["w_out_c"]], [(0, 0)], D_MODEL, tm=tm, tn=1024, res=x, name="proj_out_c")

    h = _rms(x, P["norm_ffn"][1], BF16)
    gated, nfg1, nfv1 = _ffn_up(h, P["w_up_f"], 1, P["conv_w_f"][1], P["conv_b_f"][1], ffn_buf[1],
                                seq_len, tm, 512)
    x = _mm([gated], [P["w_down_f"][1]], [(0, 0)], D_MODEL, tm=tm, tn=512, res=x, name="ffn_down")

    y_out = _rms(x, P["norm_final"], F32)
    new_ffn = jnp.stack([jnp.concatenate([nfg0, nfv0], axis=-1), jnp.concatenate([nfg1, nfv1], axis=-1)])
    return (y_out.reshape(bsz, seq_len, D_MODEL),
            new_conv[None],
            new_shift[None, :, 0, :P_B],
            new_wkv[None],
            k_rows.reshape(1, bsz, seq_len, 2 * H_C, HD_C),
            v_rows.reshape(1, bsz, seq_len, H_C, DV_C),
            new_ffn)


def kernel(x_prompt, x_sample, state_conv_a, state_shift_b, state_wkv_b, cache_k, cache_v, page_table,
           state_ffn_conv, norm_mix, norm_ffn, norm_final, w_in_ab, conv_w_a, mu_b, w0_b, w2_b, a0_b, a2_b,
           g2_b, k_k_b, k_a_b, r_k_b, ln_gain_b, ln_bias_b, w_out_ab, w_qkv_c, lambda_q1_c, lambda_k1_c,
           lambda_q2_c, lambda_k2_c, subln_w_c, w_out_c, w_up_f, conv_w_f, conv_b_f, w_down_f):
    bp, seq, d = x_prompt.shape
    bs, dec_seq, _ = x_sample.shape
    row = lambda a: a.reshape(1, -1)
    lora_rows = jnp.zeros((LORA_PAD, D_B), F32)
    P = dict(
        norm_mix=norm_mix, norm_ffn=norm_ffn, norm_final=norm_final,
        w_in_a=w_in_ab[0, :, :P_A].astype(BF16),
        w_in_b=_pad_cols(w_in_ab[0, :, P_A:], P_B_PAD).astype(BF16),
        conv_w_a=conv_w_a[0],
        mu_b=_pad_cols(row(mu_b[0]), P_B_PAD),
        w0_b=row(w0_b[0]), a0_b=row(a0_b[0]), k_k_b=row(k_k_b[0]), k_a_b=row(k_a_b[0]),
        w2_b=lora_rows.at[:R_W].set(w2_b[0]).astype(BF16),
        a2_b=lora_rows.at[R_W:R_W + R_A].set(a2_b[0]).astype(BF16),
        g2_b=lora_rows.at[R_W + R_A:R_W + R_A + R_G].set(g2_b[0]).astype(BF16),
        r_k_b=row(r_k_b[0]), ln_gain_b=row(ln_gain_b[0]), ln_bias_b=row(ln_bias_b[0]),
        w_out_ab=w_out_ab[0].astype(BF16),
        w_qkv_c=w_qkv_c[0].astype(BF16),
        lam4=jnp.stack([lambda_q1_c[0], lambda_k1_c[0], lambda_q2_c[0], lambda_k2_c[0]]),
        subln_w_c=row(subln_w_c[0]),
        w_out_c=w_out_c[0].astype(BF16),
        w_up_f=w_up_f, conv_w_f=conv_w_f, conv_b_f=conv_b_f[:, None, :],
        w_down_f=w_down_f.astype(BF16),
    )
    n_phys = cache_k.shape[1]
    outs_p = _trunk(
        x_prompt.reshape(bp * seq, d),
        jnp.zeros((bp, CONV_W - 1, D_A), F32),
        jnp.zeros((bp, 1, P_B_PAD), F32),
        jnp.zeros((bp, H_B, HD_B, HD_B), F32),
        jnp.zeros((DEPTH, bp, CONV_W - 1, 2 * D_FF), F32),
        None, None, None, P, seq)
    outs_s = _trunk(
        x_sample.reshape(bs * dec_seq, d),
        state_conv_a[0],
        _pad_cols(state_shift_b[0], P_B_PAD)[:, None, :],
        state_wkv_b[0],
        state_ffn_conv,
        cache_k.reshape(n_phys, PAGE_SIZE * 2 * H_C, HD_C),
        cache_v.reshape(n_phys, PAGE_SIZE, H_C, DV_C // LANES, LANES).swapaxes(2, 3).reshape(
            n_phys, PAGE_SIZE * H_C * DV_C // LANES, LANES),
        page_table, P, dec_seq)
    return (outs_p[0], outs_s[0]) + outs_p[1:] + outs_s[1:]
```
